```python
import math
import jax, jax.numpy as jnp
from jax import lax
import numpy as np

D_MODEL = 2048
BATCH = 4
SEQ = 4096
DEPTH = 4

N_A_LAYERS = DEPTH // 2
N_B_LAYERS = DEPTH - N_A_LAYERS
N_META = 16
D_FF = 4 * D_MODEL
GDN_HEAD_DIM = 128
GDN_QK_HEADS = D_MODEL // 128
GDN_V_HEADS = 2 * GDN_QK_HEADS
GDN_QK_DIM = GDN_QK_HEADS * GDN_HEAD_DIM
GDN_V_DIM = GDN_V_HEADS * GDN_HEAD_DIM
GDN_CONV_DIM = 2 * GDN_QK_DIM + GDN_V_DIM
GDN_IN_DIM = GDN_CONV_DIM + GDN_V_DIM + 2 * GDN_V_HEADS
GDN_CONV_K = 4
CHUNK = 64
DIFF_HEADS = D_MODEL // 256
DIFF_QK_DIM = 128
DIFF_V_DIM = 2 * DIFF_QK_DIM
DIFF_Q_DIM = DIFF_HEADS * 2 * DIFF_QK_DIM
DIFF_KV_DIM = DIFF_Q_DIM + DIFF_HEADS * DIFF_V_DIM
ROT_DIM = DIFF_QK_DIM // 4
ROPE_THETA = 500000.0
Q_BLOCK = 128
EPS = 1e-6

kernel_name = 'yoco_gdn_diffattn_hybrid'


def rms_norm(x, gain):
    x32 = x.astype(jnp.float32)
    y = x32 * lax.rsqrt(jnp.mean(x32 * x32, axis=-1, keepdims=True) + EPS)
    return y.astype(x.dtype) * gain.astype(x.dtype)


def l2_normalize(x):
    x32 = x.astype(jnp.float32)
    return (x32 * lax.rsqrt(jnp.sum(x32 * x32, axis=-1, keepdims=True) + EPS)).astype(x.dtype)


def causal_depthwise_conv(x, w):
    return lax.conv_general_dilated(
        x, w[:, None, :].astype(x.dtype), window_strides=(1,), padding=[(GDN_CONV_K - 1, 0)],
        dimension_numbers=('NWC', 'WIO', 'NWC'), feature_group_count=x.shape[-1])


def partial_rope(x, cos, sin):
    half = ROT_DIM // 2
    shape = (1, x.shape[1]) + (1,) * (x.ndim - 3) + (half,)
    c = cos.reshape(shape).astype(x.dtype)
    s = sin.reshape(shape).astype(x.dtype)
    x1, x2 = x[..., :half], x[..., half:ROT_DIM]
    return jnp.concatenate([x1 * c - x2 * s, x2 * c + x1 * s, x[..., ROT_DIM:]], axis=-1)


def chunk_gated_delta_rule(q, k, v, beta, g):
    B, T, H, DK = q.shape
    DV = v.shape[-1]
    N = T // CHUNK

    def chunks(t):
        t = t.reshape((B, N, CHUNK, H) + t.shape[3:])
        return jnp.moveaxis(t, (1, 3), (0, 2))

    tri_incl = jnp.tril(jnp.ones((CHUNK, CHUNK), dtype=bool))
    tri_strict = jnp.tril(jnp.ones((CHUNK, CHUNK), dtype=bool), -1)

    def step(S, xs):
        qc, kc, vc, bc, gc = xs
        G = jnp.cumsum(gc, axis=-1)
        decay = jnp.exp(jnp.where(tri_incl, G[..., :, None] - G[..., None, :], -jnp.inf))
        kb = kc * bc[..., None]
        a_kk = jnp.where(tri_strict, jnp.einsum('bhik,bhjk->bhij', kb, kc) * decay, 0.0)
        rhs = jnp.concatenate([vc * bc[..., None], kb * jnp.exp(G)[..., None]], axis=-1)
        sol = lax.linalg.triangular_solve(a_kk, rhs, left_side=True, lower=True, unit_diagonal=True)
        u, w = sol[..., :DV], sol[..., DV:]
        v_new = u - jnp.einsum('bhik,bhkv->bhiv', w, S)
        a_qk = jnp.einsum('bhik,bhjk->bhij', qc, kc) * decay
        o = (jnp.einsum('bhik,bhkv->bhiv', qc * jnp.exp(G)[..., None], S)
             + jnp.einsum('bhij,bhjv->bhiv', a_qk, v_new))
        g_last = G[..., -1:]
        S = (S * jnp.exp(g_last)[..., None]
             + jnp.einsum('bhik,bhiv->bhkv', kc * jnp.exp(g_last - G)[..., None], v_new))
        return S, o

    S0 = jnp.zeros((B, H, DK, DV), jnp.float32)
    _, o = lax.scan(step, S0, (chunks(q), chunks(k), chunks(v), chunks(beta), chunks(g)))
    return jnp.moveaxis(o, (0, 2), (1, 3)).reshape(B, T, H, DV)


def gated_deltanet(h, w_in, conv_w, a_log, dt_bias, o_norm, w_out):
    B, L, _ = h.shape
    proj = h @ w_in
    qkv = proj[..., :GDN_CONV_DIM]
    z = proj[..., GDN_CONV_DIM:GDN_CONV_DIM + GDN_V_DIM]
    b_logit = proj[..., GDN_CONV_DIM + GDN_V_DIM:GDN_CONV_DIM + GDN_V_DIM + GDN_V_HEADS]
    a_logit = proj[..., GDN_CONV_DIM + GDN_V_DIM + GDN_V_HEADS:]
    qkv = jax.nn.silu(causal_depthwise_conv(qkv, conv_w))
    q = qkv[..., :GDN_QK_DIM].reshape(B, L, GDN_QK_HEADS, GDN_HEAD_DIM)
    k = qkv[..., GDN_QK_DIM:2 * GDN_QK_DIM].reshape(B, L, GDN_QK_HEADS, GDN_HEAD_DIM)
    v = qkv[..., 2 * GDN_QK_DIM:].reshape(B, L, GDN_V_HEADS, GDN_HEAD_DIM)
    rep = GDN_V_HEADS // GDN_QK_HEADS
    q = jnp.repeat(l2_normalize(q) * GDN_HEAD_DIM ** -0.5, rep, axis=2)
    k = jnp.repeat(l2_normalize(k), rep, axis=2)
    beta = jax.nn.sigmoid(b_logit.astype(jnp.float32))
    g = -jnp.exp(a_log.astype(jnp.float32)) * jax.nn.softplus(
        a_logit.astype(jnp.float32) + dt_bias.astype(jnp.float32))
    lead = CHUNK - N_META

    def pad(t):
        return jnp.pad(t.astype(jnp.float32), ((0, 0), (lead, 0)) + ((0, 0),) * (t.ndim - 2))

    o = chunk_gated_delta_rule(pad(q), pad(k), pad(v), pad(beta), pad(g))[:, lead:]
    o = rms_norm(o, o_norm) * jax.nn.silu(z.reshape(B, L, GDN_V_HEADS, GDN_HEAD_DIM).astype(jnp.float32))
    return o.reshape(B, L, GDN_V_DIM).astype(h.dtype) @ w_out


def shared_kv(h, kv_norm, w_kv, cos, sin):
    B, L, _ = h.shape
    kv = rms_norm(h, kv_norm) @ w_kv
    k = partial_rope(kv[..., :DIFF_Q_DIM].reshape(B, L, DIFF_HEADS, 2, DIFF_QK_DIM), cos, sin)
    v = kv[..., DIFF_Q_DIM:].reshape(B, L, DIFF_HEADS, DIFF_V_DIM)
    return k, v


def differential_attention(h, k, v, w_q, lam, subln, w_o, lambda_init, cos, sin):
    B, L, _ = h.shape
    q = partial_rope((h @ w_q).reshape(B, L, DIFF_HEADS, 2, DIFF_QK_DIM), cos, sin) * DIFF_QK_DIM ** -0.5
    lam32 = lam.astype(jnp.float32)
    lam_val = (jnp.exp(jnp.sum(lam32[0] * lam32[1])) - jnp.exp(jnp.sum(lam32[2] * lam32[3]))
               + lambda_init)
    n_blk = -(-L // Q_BLOCK)
    Lq = n_blk * Q_BLOCK
    q = jnp.pad(q, ((0, 0), (0, Lq - L), (0, 0), (0, 0), (0, 0)))
    q = jnp.moveaxis(q.reshape(B, n_blk, Q_BLOCK, DIFF_HEADS, 2, DIFF_QK_DIM), 1, 0)
    k_pos = jnp.arange(L)

    def block(args):
        q_blk, blk = args
        s = jnp.einsum('bqhmd,bkhmd->bhmqk', q_blk, k, preferred_element_type=jnp.float32)
        q_pos = blk * Q_BLOCK + jnp.arange(Q_BLOCK)
        s = jnp.where(k_pos[None, :] <= q_pos[:, None], s, -jnp.inf)
        p = jax.nn.softmax(s, axis=-1)
        a = p[:, :, 0] - lam_val * p[:, :, 1]
        return jnp.einsum('bhqk,bkhv->bqhv', a.astype(v.dtype), v,
                          preferred_element_type=jnp.float32).astype(v.dtype)

    o = lax.map(block, (q, jnp.arange(n_blk)))
    o = jnp.moveaxis(o, 0, 1).reshape(B, Lq, DIFF_HEADS, DIFF_V_DIM)[:, :L]
    o = rms_norm(o, subln) * (1.0 - lambda_init)
    return o.reshape(B, L, DIFF_HEADS * DIFF_V_DIM) @ w_o


def squared_relu_mlp(h, w_up, w_down):
    return jnp.square(jax.nn.relu(h @ w_up)) @ w_down


def setup_inputs(seed: int = 0) -> dict:
    key = jax.random.key(seed)
    ks = jax.random.split(key, 17)
    f32 = jnp.float32

    def dense(k, shape, fan_in):
        return jax.random.normal(k, shape, f32) * fan_in ** -0.5

    def gain(k, shape):
        return 1.0 + 0.05 * jax.random.normal(k, shape, f32)

    x = jax.random.normal(ks[0], (BATCH, SEQ, D_MODEL), f32)
    meta_tokens = jax.random.normal(ks[1], (N_META, D_MODEL), f32)
    norm_gains = gain(ks[2], (DEPTH, 4, D_MODEL))
    mlp_w_up = dense(ks[3], (DEPTH, D_MODEL, D_FF), D_MODEL)
    mlp_w_down = dense(ks[4], (DEPTH, D_FF, D_MODEL), D_FF)
    gdn_w_in = dense(ks[5], (N_A_LAYERS, D_MODEL, GDN_IN_DIM), D_MODEL)
    gdn_conv_w = dense(ks[6], (N_A_LAYERS, GDN_CONV_K, GDN_CONV_DIM), GDN_CONV_K)
    gdn_a_log = jnp.log(jax.random.uniform(ks[7], (N_A_LAYERS, GDN_V_HEADS), f32, 1.0, 16.0))
    dt = jnp.exp(jax.random.uniform(ks[8], (N_A_LAYERS, GDN_V_HEADS), f32,
                                    math.log(1e-3), math.log(1e-1)))
    gdn_dt_bias = dt + jnp.log(-jnp.expm1(-dt))
    gdn_o_norm = gain(ks[9], (N_A_LAYERS, GDN_HEAD_DIM))
    gdn_w_out = dense(ks[10], (N_A_LAYERS, GDN_V_DIM, D_MODEL), GDN_V_DIM)
    kv_norm = gain(ks[11], (D_MODEL,))
    w_kv = dense(ks[12], (D_MODEL, DIFF_KV_DIM), D_MODEL)
    diff_w_q = dense(ks[13], (N_B_LAYERS, D_MODEL, DIFF_Q_DIM), D_MODEL)
    diff_lambda = 0.1 * jax.random.normal(ks[14], (N_B_LAYERS, 4, DIFF_QK_DIM), f32)
    diff_subln = gain(ks[15], (N_B_LAYERS, DIFF_V_DIM))
    diff_w_o = dense(ks[16], (N_B_LAYERS, DIFF_HEADS * DIFF_V_DIM, D_MODEL), DIFF_HEADS * DIFF_V_DIM)
    return {'x': x, 'meta_tokens': meta_tokens, 'norm_gains': norm_gains,
            'mlp_w_up': mlp_w_up, 'mlp_w_down': mlp_w_down,
            'gdn_w_in': gdn_w_in, 'gdn_conv_w': gdn_conv_w, 'gdn_a_log': gdn_a_log,
            'gdn_dt_bias': gdn_dt_bias, 'gdn_o_norm': gdn_o_norm, 'gdn_w_out': gdn_w_out,
            'kv_norm': kv_norm, 'w_kv': w_kv, 'diff_w_q': diff_w_q, 'diff_lambda': diff_lambda,
            'diff_subln': diff_subln, 'diff_w_o': diff_w_o}


def reference(x, meta_tokens, norm_gains, mlp_w_up, mlp_w_down, gdn_w_in, gdn_conv_w, gdn_a_log,
              gdn_dt_bias, gdn_o_norm, gdn_w_out, kv_norm, w_kv, diff_w_q, diff_lambda, diff_subln,
              diff_w_o):
    B = x.shape[0]
    meta = jnp.broadcast_to(meta_tokens.astype(x.dtype)[None], (B, N_META, D_MODEL))
    h = jnp.concatenate([meta, x], axis=1)
    L = h.shape[1]
    pos = jnp.arange(L, dtype=jnp.float32)
    inv_freq = ROPE_THETA ** (-jnp.arange(0, ROT_DIM, 2, dtype=jnp.float32) / ROT_DIM)
    ang = pos[:, None] * inv_freq[None, :]
    cos, sin = jnp.cos(ang), jnp.sin(ang)
    kv_k = kv_v = None
    for layer in range(DEPTH):
        hn = rms_norm(h, norm_gains[layer, 0])
        if layer < N_A_LAYERS:
            mix = gated_deltanet(hn, gdn_w_in[layer], gdn_conv_w[layer], gdn_a_log[layer],
                                 gdn_dt_bias[layer], gdn_o_norm[layer], gdn_w_out[layer])
        else:
            if layer == N_A_LAYERS:
                kv_k, kv_v = shared_kv(h, kv_norm, w_kv, cos, sin)
            j = layer - N_A_LAYERS
            lambda_init = 0.8 - 0.6 * math.exp(-0.3 * layer)
            mix = differential_attention(hn, kv_k, kv_v, diff_w_q[j], diff_lambda[j], diff_subln[j],
                                         diff_w_o[j], lambda_init, cos, sin)
        h = h + rms_norm(mix, norm_gains[layer, 1])
        ff = squared_relu_mlp(rms_norm(h, norm_gains[layer, 2]), mlp_w_up[layer], mlp_w_down[layer])
        h = h + rms_norm(ff, norm_gains[layer, 3])
    return h[:, N_META:]
```

```python
import functools
import math

import jax
import jax.numpy as jnp
from jax import lax
from jax.experimental import pallas as pl
from jax.experimental.pallas import tpu as pltpu

F32 = jnp.float32
BF16 = jnp.bfloat16

D_MODEL = 2048
DEPTH = 4
N_A_LAYERS = DEPTH // 2
N_META = 16
D_FF = 4 * D_MODEL
HEAD_DIM = 128
GDN_QK_HEADS = D_MODEL // HEAD_DIM
GDN_V_HEADS = 2 * GDN_QK_HEADS
GDN_QK_DIM = GDN_QK_HEADS * HEAD_DIM
GDN_V_DIM = GDN_V_HEADS * HEAD_DIM
GDN_CONV_DIM = 2 * GDN_QK_DIM + GDN_V_DIM
GDN_CONV_K = 4
CHUNK = 64
LEAD = CHUNK - N_META
PAIR = 2 * CHUNK
DIFF_HEADS = D_MODEL // 256
DIFF_V_DIM = 2 * HEAD_DIM
DIFF_Q_DIM = DIFF_HEADS * 2 * HEAD_DIM
ROT_DIM = HEAD_DIM // 4
ROPE_THETA = 500000.0
EPS = 1e-6
LANES = 128
ATT_BLOCK = 512
MASK_VALUE = -1e30
VMEM_LIMIT_BYTES = 56 * 1024 * 1024


def _params(*sem):
    return pltpu.CompilerParams(dimension_semantics=sem, vmem_limit_bytes=VMEM_LIMIT_BYTES)


def _pick(n, cands):
    for c in cands:
        if n % c == 0:
            return c
    return n


def _rms(x):
    return x * lax.rsqrt(jnp.mean(x * x, axis=-1, keepdims=True) + EPS)


def _rmsnorm_kernel(x_ref, g_ref, o_ref):
    o_ref[...] = (_rms(x_ref[...]) * g_ref[...]).astype(o_ref.dtype)


def rmsnorm_rows(x, gain):
    m, d = x.shape
    tm = _pick(m, (640, 512, 320, 256, 128, 64, 32, 16, 8))
    return pl.pallas_call(
        _rmsnorm_kernel,
        grid=(m // tm,),
        in_specs=[pl.BlockSpec((tm, d), lambda i: (i, 0)), pl.BlockSpec((1, d), lambda i: (0, 0))],
        out_specs=pl.BlockSpec((tm, d), lambda i: (i, 0)),
        out_shape=jax.ShapeDtypeStruct((m, d), BF16),
        compiler_params=_params("parallel"),
        name="rmsnorm_rows",
    )(x, gain.reshape(1, d))


def _resid_norm_kernel(h_ref, y_ref, gp_ref, gn_ref, *out_refs, n_next):
    h_new = h_ref[...] + _rms(y_ref[...]) * gp_ref[...]
    out_refs[0][...] = h_new
    if n_next:
        hn = _rms(h_new)
        for j in range(n_next):
            out_refs[1 + j][...] = (hn * gn_ref[j:j + 1, :]).astype(BF16)


def resid_norm(h, y, g_post, g_next):
    m, d = h.shape
    n_next = 0 if g_next is None else g_next.shape[0]
    gn = jnp.zeros((1, d), F32) if g_next is None else g_next
    tm = _pick(m, (640, 512, 320, 256, 128, 64, 32, 16, 8))
    row = pl.BlockSpec((tm, d), lambda i: (i, 0))
    outs = pl.pallas_call(
        functools.partial(_resid_norm_kernel, n_next=n_next),
        grid=(m // tm,),
        in_specs=[row, row, pl.BlockSpec((1, d), lambda i: (0, 0)),
                  pl.BlockSpec(gn.shape, lambda i: (0, 0))],
        out_specs=[row] * (1 + n_next),
        out_shape=[jax.ShapeDtypeStruct((m, d), F32)] + [jax.ShapeDtypeStruct((m, d), BF16)] * n_next,
        compiler_params=_params("parallel"),
        name="resid_norm",
    )(h, y, g_post.reshape(1, d), gn)
    return outs


def _mm_kernel(x_ref, w_ref, o_ref, *acc, nk, relu2):
    def finish(r):
        if relu2:
            r = jnp.square(jnp.maximum(r, 0.0))
        o_ref[...] = r.astype(o_ref.dtype)

    part = jnp.dot(x_ref[...], w_ref[...], preferred_element_type=F32)
    if nk == 1:
        finish(part)
        return
    acc_ref, = acc
    k = pl.program_id(2)

    @pl.when(k == 0)
    def _():
        acc_ref[...] = part

    @pl.when(k > 0)
    def _():
        acc_ref[...] += part

    @pl.when(k == nk - 1)
    def _():
        finish(acc_ref[...])


def matmul(x, w, out_dtype, relu2=False):
    m, kdim = x.shape
    n = w.shape[1]
    tm = _pick(m, (1280, 1024, 640, 512, 320, 256, 128, 64, 32, 16, 8))
    tn = _pick(n, (1024, 512, 256, 128))
    tk = _pick(kdim, (2048,))
    nk = kdim // tk
    return pl.pallas_call(
        functools.partial(_mm_kernel, nk=nk, relu2=relu2),
        grid=(m // tm, n // tn, nk),
        in_specs=[pl.BlockSpec((tm, tk), lambda i, j, k: (i, k)),
                  pl.BlockSpec((tk, tn), lambda i, j, k: (k, j))],
        out_specs=pl.BlockSpec((tm, tn), lambda i, j, k: (i, j)),
        out_shape=jax.ShapeDtypeStruct((m, n), out_dtype),
        scratch_shapes=[pltpu.VMEM((tm, tn), F32)] if nk > 1 else [],
        compiler_params=_params("parallel", "parallel", "arbitrary"),
        name="matmul",
    )(x, w)


def _gates_kernel(ba_ref, alog_ref, dtb_ref, beta_ref, gcum_ref, *, n_chunks):
    row = lax.broadcasted_iota(jnp.int32, (CHUNK, CHUNK), 0)
    col = lax.broadcasted_iota(jnp.int32, (CHUNK, CHUNK), 1)
    tril = (row >= col).astype(F32)
    neg_rate = -jnp.exp(alog_ref[...])
    dtb = dtb_ref[...]

    def body(c, carry):
        r0 = pl.multiple_of(c * CHUNK, CHUNK)
        ba = ba_ref[0, pl.ds(r0, CHUNK), :]
        b, a = ba[:, :LANES], ba[:, LANES:]
        pos = r0 + lax.broadcasted_iota(jnp.int32, (CHUNK, LANES), 0)
        live = pos >= LEAD
        x = a + dtb
        softplus = jnp.maximum(x, 0.0) + jnp.log(1.0 + jnp.exp(-jnp.abs(x)))
        g = jnp.where(live, neg_rate * softplus, 0.0)
        beta_ref[0, pl.ds(r0, CHUNK), :] = jnp.where(live, 1.0 / (1.0 + jnp.exp(-b)), 0.0)
        gcum_ref[0, pl.ds(r0, CHUNK), :] = jnp.dot(tril, g, precision=lax.Precision.HIGHEST,
                                                   preferred_element_type=F32)
        return carry

    lax.fori_loop(0, n_chunks, body, 0)


def gdn_gates(ba, a_log, dt_bias):
    b, lp, _ = ba.shape
    pad = lambda t: jnp.zeros((1, LANES), F32).at[0, :GDN_V_HEADS].set(t.astype(F32))
    blk = pl.BlockSpec((1, lp, LANES), lambda i: (i, 0, 0))
    return pl.pallas_call(
        functools.partial(_gates_kernel, n_chunks=lp // CHUNK),
        grid=(b,),
        in_specs=[pl.BlockSpec((1, lp, 2 * LANES), lambda i: (i, 0, 0)),
                  pl.BlockSpec((1, LANES), lambda i: (0, 0)), pl.BlockSpec((1, LANES), lambda i: (0, 0))],
        out_specs=[blk, blk],
        out_shape=[jax.ShapeDtypeStruct((b, lp, LANES), F32)] * 2,
        compiler_params=_params("parallel"),
        name="gdn_gates",
    )(ba, pad(a_log), pad(dt_bias))


def _silu(x):
    return x / (1.0 + jnp.exp(-x))


def _causal_conv_silu(x_ref, w_ref, r0, first):
    width = x_ref.shape[-1]
    x = x_ref[0, pl.ds(r0, CHUNK), :].astype(F32)
    p0 = pl.multiple_of(jnp.maximum(r0 - 16, 0), 16)
    prev = x_ref[0, pl.ds(p0, 16), :].astype(F32)[8:, :]
    prev = prev * jnp.where(first, 0.0, 1.0)
    w = w_ref[...]
    row = lax.broadcasted_iota(jnp.int32, (CHUNK, width), 0)
    y = x * w[GDN_CONV_K - 1:GDN_CONV_K, :]
    for s in range(1, GDN_CONV_K):
        xs = pltpu.roll(x, s, 0)
        ps = jnp.concatenate([pltpu.roll(prev, s, 0)] * (CHUNK // 8), axis=0)
        xs = jnp.where(row < s, ps, xs)
        y = y + xs * w[GDN_CONV_K - 1 - s:GDN_CONV_K - s, :]
    return _silu(y)


def _gdn_kernel(q_ref, k_ref, v_ref, z_ref, wq_ref, wk_ref, wv_ref, bcol_ref, gcol_ref, grow_ref,
                onorm_ref, o_ref, s_ref, u_s, wq_s, aqk_s, kdt_s, *, n_chunks, unroll):
    ri = lax.broadcasted_iota(jnp.int32, (PAIR, PAIR), 0)
    ci = lax.broadcasted_iota(jnp.int32, (PAIR, PAIR), 1)
    same_head = (ri // CHUNK) == (ci // CHUNK)
    incl = same_head & (ri >= ci)
    strict = same_head & (ri > ci)
    eye = (ri == ci).astype(F32)

    def dot_t(a, b):
        return lax.dot_general(a, b, (((1,), (1,)), ((), ())), preferred_element_type=F32)

    def prepare(c, carry):
        r0 = pl.multiple_of(c * CHUNK, CHUNK)
        first = c == 0
        q = _causal_conv_silu(q_ref, wq_ref, r0, first)
        k = _causal_conv_silu(k_ref, wk_ref, r0, first)
        v = _causal_conv_silu(v_ref, wv_ref, r0, first)
        q = q * lax.rsqrt(jnp.sum(q * q, axis=-1, keepdims=True) + EPS) * HEAD_DIM ** -0.5
        k = k * lax.rsqrt(jnp.sum(k * k, axis=-1, keepdims=True) + EPS)
        bc = bcol_ref[0, 0, pl.ds(r0, CHUNK), :]
        gc = gcol_ref[0, 0, pl.ds(r0, CHUNK), :]
        g_row = grow_ref[0, 0, pl.ds(c, 1), :]
        g_col = jnp.concatenate([gc[:, 0:1], gc[:, 1:2]], axis=0)
        beta = jnp.concatenate([bc[:, 0:1], bc[:, 1:2]], axis=0)
        g_last = jnp.concatenate([jnp.broadcast_to(gc[CHUNK - 1:, 0:1], (CHUNK, 1)),
                                  jnp.broadcast_to(gc[CHUNK - 1:, 1:2], (CHUNK, 1))], axis=0)
        decay = jnp.exp(jnp.where(incl, g_col - g_row, -jnp.inf))
        k2 = jnp.concatenate([k, k], axis=0)
        q2 = jnp.concatenate([q, q], axis=0)
        kb = k2 * beta
        k2b = k2.astype(BF16)
        a_kk = jnp.where(strict, dot_t(kb.astype(BF16), k2b) * decay, 0.0)
        aqk_s[c] = (dot_t(q2.astype(BF16), k2b) * decay).astype(BF16)
        t = eye
        blk = 1
        while blk < CHUNK:
            f = jnp.where(((ri // blk) % 2 == 1) & ((ci // blk) % 2 == 0)
                          & ((ri // (2 * blk)) == (ci // (2 * blk))), a_kk, 0.0).astype(BF16)
            tb = t.astype(BF16)
            ft = jnp.dot(f, tb, preferred_element_type=F32)
            t = t - jnp.dot(tb, ft.astype(BF16), preferred_element_type=F32)
            blk *= 2
        e_g = jnp.exp(g_col)
        v2 = jnp.concatenate([v[:, :HEAD_DIM], v[:, HEAD_DIM:]], axis=0)
        rhs = jnp.concatenate([v2 * beta, kb * e_g], axis=1).astype(BF16)
        sol = jnp.dot(t.astype(BF16), rhs, preferred_element_type=F32)
        u_s[c] = sol[:, :HEAD_DIM]
        w = sol[:, HEAD_DIM:].astype(BF16)
        qe = (q2 * e_g).astype(BF16)
        wq_s[c, 0] = jnp.concatenate([w[:CHUNK], qe[:CHUNK]], axis=0)
        wq_s[c, 1] = jnp.concatenate([w[CHUNK:], qe[CHUNK:]], axis=0)
        kdt_s[c] = (k2 * jnp.exp(g_last - g_col)).T.astype(BF16)
        return carry

    lax.fori_loop(0, n_chunks, prepare, 0, unroll=unroll)

    s_ref[...] = jnp.zeros_like(s_ref)
    lane_head = lax.broadcasted_iota(jnp.int32, (HEAD_DIM, PAIR), 1) // CHUNK
    onorm = onorm_ref[...]

    def recur(c, carry):
        r0 = pl.multiple_of(c * CHUNK, CHUNK)
        gc = gcol_ref[0, 0, pl.ds(r0 + CHUNK - 8, 8), :]
        u = u_s[c]
        r = [jnp.dot(wq_s[c, hh], s_ref[hh].astype(BF16), preferred_element_type=F32) for hh in range(2)]
        v_new = jnp.concatenate([u[:CHUNK] - r[0][:CHUNK], u[CHUNK:] - r[1][:CHUNK]], axis=0)
        v_new_b = v_new.astype(BF16)
        o_intra = jnp.dot(aqk_s[c], v_new_b, preferred_element_type=F32)
        kdt = kdt_s[c]
        z = z_ref[0, pl.ds(r0, CHUNK), :].astype(F32)
        for hh in range(2):
            kdt_h = jnp.where(lane_head == hh, kdt, jnp.zeros_like(kdt))
            s_ref[hh] = (s_ref[hh] * jnp.exp(gc[7:8, hh:hh + 1])
                         + jnp.dot(kdt_h, v_new_b, preferred_element_type=F32))
            o = r[hh][CHUNK:] + o_intra[hh * CHUNK:(hh + 1) * CHUNK]
            o = _rms(o) * onorm * _silu(z[:, hh * HEAD_DIM:(hh + 1) * HEAD_DIM])
            o_ref[0, pl.ds(r0, CHUNK), hh * HEAD_DIM:(hh + 1) * HEAD_DIM] = o.astype(o_ref.dtype)
        return carry

    lax.fori_loop(0, n_chunks, recur, 0)


def gdn_mix(qkv, z, conv_w, beta, gcum, o_norm):
    b, lp, _ = qkv.shape
    nc = lp // CHUNK
    nqk = GDN_QK_HEADS
    col = lambda t: t.reshape(b, lp, nqk, 2).transpose(0, 2, 1, 3)
    grow = gcum.reshape(b, nc, CHUNK, nqk, 2).transpose(0, 3, 1, 4, 2).reshape(b, nqk, nc, PAIR)
    seq = lambda w, off: pl.BlockSpec((1, lp, w), lambda i, j, off=off: (i, 0, off + j))
    cw = lambda w, off: pl.BlockSpec((GDN_CONV_K, w), lambda i, j, off=off: (0, off + j))
    gate = pl.BlockSpec((1, 1, lp, 2), lambda i, j: (i, j, 0, 0))
    unroll = _pick(nc, (5, 3, 2, 1))
    return pl.pallas_call(
        functools.partial(_gdn_kernel, n_chunks=nc, unroll=unroll),
        grid=(b, nqk),
        in_specs=[seq(HEAD_DIM, 0), seq(HEAD_DIM, nqk), seq(2 * HEAD_DIM, nqk), seq(2 * HEAD_DIM, 0),
                  cw(HEAD_DIM, 0), cw(HEAD_DIM, nqk), cw(2 * HEAD_DIM, nqk),
                  gate, gate, pl.BlockSpec((1, 1, nc, PAIR), lambda i, j: (i, j, 0, 0)),
                  pl.BlockSpec((1, HEAD_DIM), lambda i, j: (0, 0))],
        out_specs=pl.BlockSpec((1, lp, 2 * HEAD_DIM), lambda i, j: (i, 0, j)),
        out_shape=jax.ShapeDtypeStruct((b, lp, GDN_V_DIM), BF16),
        scratch_shapes=[pltpu.VMEM((2, HEAD_DIM, HEAD_DIM), F32),
                        pltpu.VMEM((nc, PAIR, HEAD_DIM), F32),
                        pltpu.VMEM((nc, 2, PAIR, HEAD_DIM), BF16),
                        pltpu.VMEM((nc, PAIR, PAIR), BF16),
                        pltpu.VMEM((nc, HEAD_DIM, PAIR), BF16)],
        compiler_params=_params("parallel", "parallel"),
        name="gdn_mix",
    )(qkv, qkv, qkv, z, conv_w, conv_w, conv_w, col(beta), col(gcum), grow, o_norm.reshape(1, HEAD_DIM))


def _rope(x, cos, sin_lo, sin_hi):
    half = ROT_DIM // 2
    return x * cos + pltpu.roll(x, half, 1) * sin_hi + pltpu.roll(x, HEAD_DIM - half, 1) * sin_lo


def _attn_kernel(q_ref, k_ref, v_ref, cos_ref, slo_ref, shi_ref, lam_ref, subln_ref, o_ref,
                 kr_s, m_s, l_s, acc_s, *, n_blocks, lambda_init):
    tb = ATT_BLOCK
    lam = lam_ref[...]
    lam_val = (jnp.exp(jnp.sum(lam[0:1] * lam[1:2], axis=-1, keepdims=True))
               - jnp.exp(jnp.sum(lam[2:3] * lam[3:4], axis=-1, keepdims=True)) + lambda_init)
    subln = subln_ref[...]

    def roped(ref, r0, n, scale):
        x = ref[0, pl.ds(r0, n), :].astype(F32)
        tabs = (cos_ref[pl.ds(r0, n), :], slo_ref[pl.ds(r0, n), :], shi_ref[pl.ds(r0, n), :])
        return [(_rope(x[:, m * HEAD_DIM:(m + 1) * HEAD_DIM], *tabs) * scale).astype(BF16) for m in range(2)]

    def dot_t(a, b):
        return lax.dot_general(a, b, (((1,), (1,)), ((), ())), preferred_element_type=F32)

    k_rows = [(0, CHUNK)] + [(CHUNK + i * tb, tb) for i in range(n_blocks)]
    for r0, n in k_rows:
        k1, k2 = roped(k_ref, r0, n, 1.0)
        kr_s[pl.ds(r0, n), :HEAD_DIM] = k1
        kr_s[pl.ds(r0, n), HEAD_DIM:] = k2

    def start(qm, nq, qrow0, causal_head):
        kpos = lax.broadcasted_iota(jnp.int32, (nq, CHUNK), 1)
        ok = kpos >= LEAD
        if causal_head:
            ok = ok & (kpos <= qrow0 + lax.broadcasted_iota(jnp.int32, (nq, CHUNK), 0))
        vh = v_ref[0, pl.ds(0, CHUNK), :]
        for m in range(2):
            s = jnp.where(ok, dot_t(qm[m], kr_s[pl.ds(0, CHUNK), m * HEAD_DIM:(m + 1) * HEAD_DIM]), MASK_VALUE)
            mx = jnp.max(s, axis=-1, keepdims=True)
            p = jnp.exp(s - mx)
            m_s[m, pl.ds(0, nq), :] = mx
            l_s[m, pl.ds(0, nq), :] = jnp.sum(p, axis=-1, keepdims=True)
            acc_s[m, pl.ds(0, nq), :] = jnp.dot(p.astype(BF16), vh, preferred_element_type=F32)

    def update(qm, r0, diag):
        vb = v_ref[0, pl.ds(r0, tb), :]
        if diag:
            ok = (lax.broadcasted_iota(jnp.int32, (tb, tb), 1) <= lax.broadcasted_iota(jnp.int32, (tb, tb), 0))
        for m in range(2):
            s = dot_t(qm[m], kr_s[pl.ds(r0, tb), m * HEAD_DIM:(m + 1) * HEAD_DIM])
            if diag:
                s = jnp.where(ok, s, MASK_VALUE)
            m_old = m_s[m]
            m_new = jnp.maximum(m_old, jnp.max(s, axis=-1, keepdims=True))
            alpha = jnp.exp(m_old - m_new)
            p = jnp.exp(s - m_new)
            m_s[m] = m_new
            l_s[m] = alpha * l_s[m] + jnp.sum(p, axis=-1, keepdims=True)
            acc_s[m] = alpha * acc_s[m] + jnp.dot(p.astype(BF16), vb, preferred_element_type=F32)

    def finish(r0, nq):
        o = (acc_s[0, pl.ds(0, nq), :] / l_s[0, pl.ds(0, nq), :]
             - lam_val * (acc_s[1, pl.ds(0, nq), :] / l_s[1, pl.ds(0, nq), :]))
        o_ref[0, pl.ds(r0, nq), :] = (_rms(o) * subln * (1.0 - lambda_init)).astype(o_ref.dtype)

    scale = HEAD_DIM ** -0.5
    start(roped(q_ref, 0, CHUNK, scale), CHUNK, 0, True)
    finish(0, CHUNK)

    def q_block(qi, carry):
        r0 = pl.multiple_of(CHUNK + qi * tb, CHUNK)
        qm = roped(q_ref, r0, tb, scale)
        start(qm, tb, r0, False)

        def kv_block(j, c2):
            update(qm, pl.multiple_of(CHUNK + j * tb, CHUNK), False)
            return c2

        lax.fori_loop(0, qi, kv_block, 0)
        update(qm, r0, True)
        finish(r0, tb)
        return carry

    lax.fori_loop(0, n_blocks, q_block, 0)


def diff_attention(q, k, v, rope_tabs, lam, subln, lambda_init):
    b, lp, _ = q.shape
    n_blocks = (lp - CHUNK) // ATT_BLOCK
    assert CHUNK + n_blocks * ATT_BLOCK == lp
    head = pl.BlockSpec((1, lp, 2 * HEAD_DIM), lambda i, j: (i, 0, j))
    tab = pl.BlockSpec((lp, HEAD_DIM), lambda i, j: (0, 0))
    return pl.pallas_call(
        functools.partial(_attn_kernel, n_blocks=n_blocks, lambda_init=lambda_init),
        grid=(b, DIFF_HEADS),
        in_specs=[head, head, head, tab, tab, tab,
                  pl.BlockSpec((4, HEAD_DIM), lambda i, j: (0, 0)),
                  pl.BlockSpec((1, DIFF_V_DIM), lambda i, j: (0, 0))],
        out_specs=head,
        out_shape=jax.ShapeDtypeStruct((b, lp, DIFF_HEADS * DIFF_V_DIM), BF16),
        scratch_shapes=[pltpu.VMEM((lp, 2 * HEAD_DIM), BF16),
                        pltpu.VMEM((2, ATT_BLOCK, 1), F32),
                        pltpu.VMEM((2, ATT_BLOCK, 1), F32),
                        pltpu.VMEM((2, ATT_BLOCK, DIFF_V_DIM), F32)],
        compiler_params=_params("parallel", "parallel"),
        name="diff_attention",
    )(q, k, v, *rope_tabs, lam, subln.reshape(1, DIFF_V_DIM))


def _rope_tables(lp):
    half = ROT_DIM // 2
    pos = jnp.maximum(jnp.arange(lp, dtype=F32) - LEAD, 0.0)
    inv_freq = ROPE_THETA ** (-jnp.arange(0, ROT_DIM, 2, dtype=F32) / ROT_DIM)
    ang = pos[:, None] * inv_freq[None, :]
    c, s = jnp.cos(ang), jnp.sin(ang)
    zeros = jnp.zeros((lp, HEAD_DIM - ROT_DIM), F32)
    cos = jnp.concatenate([c, c, jnp.ones_like(zeros)], axis=1)
    sin_lo = jnp.concatenate([-s, jnp.zeros((lp, half), F32), zeros], axis=1)
    sin_hi = jnp.concatenate([jnp.zeros((lp, half), F32), s, zeros], axis=1)
    return cos, sin_lo, sin_hi


def kernel(x, meta_tokens, norm_gains, mlp_w_up, mlp_w_down, gdn_w_in, gdn_conv_w, gdn_a_log, gdn_dt_bias,
           gdn_o_norm, gdn_w_out, kv_norm, w_kv, diff_w_q, diff_lambda, diff_subln, diff_w_o):
    b, seq, d = x.shape
    lp = LEAD + N_META + seq
    m = b * lp
    meta = jnp.broadcast_to(meta_tokens.astype(x.dtype)[None], (b, N_META, d))
    h = jnp.concatenate([jnp.zeros((b, LEAD, d), x.dtype), meta, x], axis=1).reshape(m, d)
    rope_tabs = _rope_tables(lp)
    wb = lambda w: w.astype(BF16)

    hn = rmsnorm_rows(h, norm_gains[0, 0])
    kv_k = kv_v = None
    for layer in range(DEPTH):
        if layer < N_A_LAYERS:
            w_in = gdn_w_in[layer]
            qkv = matmul(hn, wb(w_in[:, :GDN_CONV_DIM]), BF16)
            z = matmul(hn, wb(w_in[:, GDN_CONV_DIM:GDN_CONV_DIM + GDN_V_DIM]), BF16)
            w_ba = jnp.zeros((d, 2 * LANES), F32)
            w_ba = w_ba.at[:, :GDN_V_HEADS].set(w_in[:, GDN_CONV_DIM + GDN_V_DIM:GDN_CONV_DIM + GDN_V_DIM + GDN_V_HEADS])
            w_ba = w_ba.at[:, LANES:LANES + GDN_V_HEADS].set(w_in[:, GDN_CONV_DIM + GDN_V_DIM + GDN_V_HEADS:])
            ba = matmul(hn, wb(w_ba), F32)
            beta, gcum = gdn_gates(ba.reshape(b, lp, 2 * LANES), gdn_a_log[layer], gdn_dt_bias[layer])
            o = gdn_mix(qkv.reshape(b, lp, GDN_CONV_DIM), z.reshape(b, lp, GDN_V_DIM), gdn_conv_w[layer],
                        beta[..., :GDN_V_HEADS], gcum[..., :GDN_V_HEADS], gdn_o_norm[layer])
            mix = matmul(o.reshape(m, GDN_V_DIM), wb(gdn_w_out[layer]), F32)
        else:
            j = layer - N_A_LAYERS
            lambda_init = 0.8 - 0.6 * math.exp(-0.3 * layer)
            q = matmul(hn, wb(diff_w_q[j]), BF16)
            o = diff_attention(q.reshape(b, lp, DIFF_Q_DIM), kv_k, kv_v, rope_tabs, diff_lambda[j],
                               diff_subln[j], lambda_init)
            mix = matmul(o.reshape(m, DIFF_HEADS * DIFF_V_DIM), wb(diff_w_o[j]), F32)
        h, hn = resid_norm(h, mix, norm_gains[layer, 1], norm_gains[layer, 2:3])
        up = matmul(hn, wb(mlp_w_up[layer]), BF16, relu2=True)
        ff = matmul(up, wb(mlp_w_down[layer]), F32)
        if layer == N_A_LAYERS - 1:
            g_next = jnp.stack([norm_gains[layer + 1, 0], kv_norm])
            h, hn, hkv = resid_norm(h, ff, norm_gains[layer, 3], g_next)
            kv_k = matmul(hkv, wb(w_kv[:, :DIFF_Q_DIM]), BF16).reshape(b, lp, DIFF_Q_DIM)
            kv_v = matmul(hkv, wb(w_kv[:, DIFF_Q_DIM:]), BF16).reshape(b, lp, DIFF_HEADS * DIFF_V_DIM)
        elif layer + 1 < DEPTH:
            h, hn = resid_norm(h, ff, norm_gains[layer, 3], norm_gains[layer + 1, 0:1])
        else:
            h, = resid_norm(h, ff, norm_gains[layer, 3], None)
    return h.reshape(b, lp, d)[:, LEAD + N_META:]
```

```python
import functools
import math

import jax
import jax.numpy as jnp
from jax import lax
from jax.experimental import pallas as pl
from jax.experimental.pallas import tpu as pltpu

F32 = jnp.float32
BF16 = jnp.bfloat16

D_MODEL = 2048
DEPTH = 4
N_A_LAYERS = DEPTH // 2
N_META = 16
D_FF = 4 * D_MODEL
HEAD_DIM = 128
GDN_QK_HEADS = D_MODEL // HEAD_DIM
GDN_V_HEADS = 2 * GDN_QK_HEADS
GDN_QK_DIM = GDN_QK_HEADS * HEAD_DIM
GDN_V_DIM = GDN_V_HEADS * HEAD_DIM
GDN_CONV_DIM = 2 * GDN_QK_DIM + GDN_V_DIM
GDN_CONV_K = 4
CHUNK = 64
LEAD = CHUNK - N_META
PAIR = 2 * CHUNK
DIFF_HEADS = D_MODEL // 256
DIFF_V_DIM = 2 * HEAD_DIM
DIFF_Q_DIM = DIFF_HEADS * 2 * HEAD_DIM
ROT_DIM = HEAD_DIM // 4
ROPE_THETA = 500000.0
EPS = 1e-6
LANES = 128
GDN_PAIRS = 2
GDN_BLOCK_CHUNKS = 5
ATT_BLOCK = 512
MASK_VALUE = -1e30
VMEM_LIMIT_BYTES = 56 * 1024 * 1024


def _params(*sem):
    return pltpu.CompilerParams(dimension_semantics=sem, vmem_limit_bytes=VMEM_LIMIT_BYTES)


def _pick(n, cands):
    for c in cands:
        if n % c == 0:
            return c
    return n


def _rms(x):
    return x * lax.rsqrt(jnp.mean(x * x, axis=-1, keepdims=True) + EPS)


def _rmsnorm_kernel(x_ref, g_ref, o_ref):
    o_ref[...] = (_rms(x_ref[...]) * g_ref[...]).astype(o_ref.dtype)


def rmsnorm_rows(x, gain):
    m, d = x.shape
    tm = _pick(m, (640, 512, 320, 256, 128, 64, 32, 16, 8))
    return pl.pallas_call(
        _rmsnorm_kernel,
        grid=(m // tm,),
        in_specs=[pl.BlockSpec((tm, d), lambda i: (i, 0)), pl.BlockSpec((1, d), lambda i: (0, 0))],
        out_specs=pl.BlockSpec((tm, d), lambda i: (i, 0)),
        out_shape=jax.ShapeDtypeStruct((m, d), BF16),
        compiler_params=_params("parallel"),
        name="rmsnorm_rows",
    )(x, gain.reshape(1, d))


def _resid_norm_kernel(h_ref, y_ref, gp_ref, gn_ref, *out_refs, n_next):
    h_new = h_ref[...] + _rms(y_ref[...]) * gp_ref[...]
    out_refs[0][...] = h_new
    if n_next:
        hn = _rms(h_new)
        for j in range(n_next):
            out_refs[1 + j][...] = (hn * gn_ref[j:j + 1, :]).astype(BF16)


def resid_norm(h, y, g_post, g_next):
    m, d = h.shape
    n_next = 0 if g_next is None else g_next.shape[0]
    gn = jnp.zeros((1, d), F32) if g_next is None else g_next
    tm = _pick(m, (640, 512, 320, 256, 128, 64, 32, 16, 8))
    row = pl.BlockSpec((tm, d), lambda i: (i, 0))
    outs = pl.pallas_call(
        functools.partial(_resid_norm_kernel, n_next=n_next),
        grid=(m // tm,),
        in_specs=[row, row, pl.BlockSpec((1, d), lambda i: (0, 0)),
                  pl.BlockSpec(gn.shape, lambda i: (0, 0))],
        out_specs=[row] * (1 + n_next),
        out_shape=[jax.ShapeDtypeStruct((m, d), F32)] + [jax.ShapeDtypeStruct((m, d), BF16)] * n_next,
        compiler_params=_params("parallel"),
        name="resid_norm",
    )(h, y, g_post.reshape(1, d), gn)
    return outs


def _mm_kernel(x_ref, w_ref, o_ref, *acc, nk, relu2):
    def finish(r):
        if relu2:
            r = jnp.square(jnp.maximum(r, 0.0))
        o_ref[...] = r.astype(o_ref.dtype)

    part = jnp.dot(x_ref[...], w_ref[...], preferred_element_type=F32)
    if nk == 1:
        finish(part)
        return
    acc_ref, = acc
    k = pl.program_id(2)

    @pl.when(k == 0)
    def _():
        acc_ref[...] = part

    @pl.when(k > 0)
    def _():
        acc_ref[...] += part

    @pl.when(k == nk - 1)
    def _():
        finish(acc_ref[...])


def matmul(x, w, out_dtype, relu2=False):
    m, kdim = x.shape
    n = w.shape[1]
    tm = _pick(m, (1280, 1024, 640, 512, 320, 256, 128, 64, 32, 16, 8))
    tn = _pick(n, (1024, 512, 256, 128))
    tk = _pick(kdim, (2048,))
    nk = kdim // tk
    return pl.pallas_call(
        functools.partial(_mm_kernel, nk=nk, relu2=relu2),
        grid=(m // tm, n // tn, nk),
        in_specs=[pl.BlockSpec((tm, tk), lambda i, j, k: (i, k)),
                  pl.BlockSpec((tk, tn), lambda i, j, k: (k, j))],
        out_specs=pl.BlockSpec((tm, tn), lambda i, j, k: (i, j)),
        out_shape=jax.ShapeDtypeStruct((m, n), out_dtype),
        scratch_shapes=[pltpu.VMEM((tm, tn), F32)] if nk > 1 else [],
        compiler_params=_params("parallel", "parallel", "arbitrary"),
        name="matmul",
    )(x, w)


def _gates_kernel(ba_ref, alog_ref, dtb_ref, beta_ref, gcum_ref, *, n_chunks):
    row = lax.broadcasted_iota(jnp.int32, (CHUNK, CHUNK), 0)
    col = lax.broadcasted_iota(jnp.int32, (CHUNK, CHUNK), 1)
    tril = (row >= col).astype(F32)
    neg_rate = -jnp.exp(alog_ref[...])
    dtb = dtb_ref[...]

    def body(c, carry):
        r0 = pl.multiple_of(c * CHUNK, CHUNK)
        ba = ba_ref[0, pl.ds(r0, CHUNK), :]
        b, a = ba[:, :LANES], ba[:, LANES:]
        pos = r0 + lax.broadcasted_iota(jnp.int32, (CHUNK, LANES), 0)
        live = pos >= LEAD
        x = a + dtb
        softplus = jnp.maximum(x, 0.0) + jnp.log(1.0 + jnp.exp(-jnp.abs(x)))
        g = jnp.where(live, neg_rate * softplus, 0.0)
        beta_ref[0, pl.ds(r0, CHUNK), :] = jnp.where(live, 1.0 / (1.0 + jnp.exp(-b)), 0.0)
        gcum_ref[0, pl.ds(r0, CHUNK), :] = jnp.dot(tril, g, precision=lax.Precision.HIGHEST,
                                                   preferred_element_type=F32)
        return carry

    lax.fori_loop(0, n_chunks, body, 0)


def gdn_gates(ba, a_log, dt_bias):
    b, lp, _ = ba.shape
    pad = lambda t: jnp.zeros((1, LANES), F32).at[0, :GDN_V_HEADS].set(t.astype(F32))
    blk = pl.BlockSpec((1, lp, LANES), lambda i: (i, 0, 0))
    return pl.pallas_call(
        functools.partial(_gates_kernel, n_chunks=lp // CHUNK),
        grid=(b,),
        in_specs=[pl.BlockSpec((1, lp, 2 * LANES), lambda i: (i, 0, 0)),
                  pl.BlockSpec((1, LANES), lambda i: (0, 0)), pl.BlockSpec((1, LANES), lambda i: (0, 0))],
        out_specs=[blk, blk],
        out_shape=[jax.ShapeDtypeStruct((b, lp, LANES), F32)] * 2,
        compiler_params=_params("parallel"),
        name="gdn_gates",
    )(ba, pad(a_log), pad(dt_bias))


def _silu(x):
    return x / (1.0 + jnp.exp(-x))


def _conv_silu(x, prev, w):
    row = lax.broadcasted_iota(jnp.int32, x.shape, 0)
    y = x * w[GDN_CONV_K - 1:GDN_CONV_K, :]
    for s in range(1, GDN_CONV_K):
        ps = jnp.concatenate([pltpu.roll(prev, s, 0)] * (CHUNK // 8), axis=0)
        xs = jnp.where(row < s, ps, pltpu.roll(x, s, 0))
        y = y + xs * w[GDN_CONV_K - 1 - s:GDN_CONV_K - s, :]
    return _silu(y)


def _gdn_kernel(q_ref, k_ref, v_ref, z_ref, qp_ref, kp_ref, vp_ref, wq_ref, wk_ref, wv_ref, bcol_ref, gcol_ref,
                grow_ref, onorm_ref, o_ref, s_ref, u_s, wq_s, aqk_s, kdt_s):
    t_blk = pl.program_id(2)
    ri = lax.broadcasted_iota(jnp.int32, (PAIR, PAIR), 0)
    ci = lax.broadcasted_iota(jnp.int32, (PAIR, PAIR), 1)
    same_head = (ri // CHUNK) == (ci // CHUNK)
    incl = same_head & (ri >= ci)
    strict = same_head & (ri > ci)
    eye = (ri == ci).astype(F32)
    seen_rows = jnp.where(t_blk > 0, 1.0, 0.0)

    @pl.when(t_blk == 0)
    def _():
        s_ref[...] = jnp.zeros_like(s_ref)

    def dot(a, b):
        return jnp.dot(a, b, preferred_element_type=F32)

    def dot_t(a, b):
        return lax.dot_general(a, b, (((1,), (1,)), ((), ())), preferred_element_type=F32)

    def conv_part(ref, halo_ref, w_ref, c, lo, width):
        x = ref[0, c * CHUNK:(c + 1) * CHUNK, lo:lo + width].astype(F32)
        if c:
            prev = ref[0, c * CHUNK - 16:c * CHUNK, lo:lo + width].astype(F32)[8:, :]
        else:
            prev = halo_ref[0, :, lo:lo + width].astype(F32)[8:, :] * seen_rows
        return _conv_silu(x, prev, w_ref[:, lo:lo + width])

    units = [(c, p) for c in range(GDN_BLOCK_CHUNKS) for p in range(GDN_PAIRS)]

    a_kk, k2s, q2s, kbs, v2s, betas, g_cols, g_lasts = [], [], [], [], [], [], [], []
    for n, (c, p) in enumerate(units):
        q = conv_part(q_ref, qp_ref, wq_ref, c, p * HEAD_DIM, HEAD_DIM)
        k = conv_part(k_ref, kp_ref, wk_ref, c, p * HEAD_DIM, HEAD_DIM)
        v = conv_part(v_ref, vp_ref, wv_ref, c, 2 * p * HEAD_DIM, 2 * HEAD_DIM)
        q = q * lax.rsqrt(jnp.sum(q * q, axis=-1, keepdims=True) + EPS) * HEAD_DIM ** -0.5
        k = k * lax.rsqrt(jnp.sum(k * k, axis=-1, keepdims=True) + EPS)
        bc = bcol_ref[0, p, c * CHUNK:(c + 1) * CHUNK, :]
        gc = gcol_ref[0, p, c * CHUNK:(c + 1) * CHUNK, :]
        g_row = grow_ref[0, p, 0, c:c + 1, :]
        g_col = jnp.concatenate([gc[:, 0:1], gc[:, 1:2]], axis=0)
        beta = jnp.concatenate([bc[:, 0:1], bc[:, 1:2]], axis=0)
        g_last = jnp.concatenate([jnp.broadcast_to(gc[CHUNK - 1:, 0:1], (CHUNK, 1)),
                                  jnp.broadcast_to(gc[CHUNK - 1:, 1:2], (CHUNK, 1))], axis=0)
        decay = jnp.exp(jnp.where(incl, g_col - g_row, -jnp.inf))
        k2 = jnp.concatenate([k, k], axis=0)
        q2 = jnp.concatenate([q, q], axis=0)
        kb = k2 * beta
        k2b = k2.astype(BF16)
        a_kk.append(jnp.where(strict, dot_t(kb.astype(BF16), k2b) * decay, 0.0))
        aqk_s[n] = (dot_t(q2.astype(BF16), k2b) * decay).astype(BF16)
        k2s.append(k2)
        q2s.append(q2)
        kbs.append(kb)
        betas.append(beta)
        v2s.append(jnp.concatenate([v[:, :HEAD_DIM], v[:, HEAD_DIM:]], axis=0))
        g_cols.append(g_col)
        g_lasts.append(g_last)

    def sub_blocks(blk):
        return (((ri // blk) % 2 == 1) & ((ci // blk) % 2 == 0) & ((ri // (2 * blk)) == (ci // (2 * blk))))

    ts = [eye - jnp.where(sub_blocks(1), a, 0.0) for a in a_kk]
    blk = 2
    while blk < CHUNK:
        mask = sub_blocks(blk)
        tbs = [t.astype(BF16) for t in ts]
        fts = [dot(jnp.where(mask, a, 0.0).astype(BF16), tb).astype(BF16) for a, tb in zip(a_kk, tbs)]
        ts = [t - dot(tb, ft) for t, tb, ft in zip(ts, tbs, fts)]
        blk *= 2

    for n in range(len(units)):
        e_g = jnp.exp(g_cols[n])
        rhs = jnp.concatenate([v2s[n] * betas[n], kbs[n] * e_g], axis=1).astype(BF16)
        sol = dot(ts[n].astype(BF16), rhs)
        u_s[n] = sol[:, :HEAD_DIM]
        w = sol[:, HEAD_DIM:].astype(BF16)
        qe = (q2s[n] * e_g).astype(BF16)
        wq_s[n, 0] = jnp.concatenate([w[:CHUNK], qe[:CHUNK]], axis=0)
        wq_s[n, 1] = jnp.concatenate([w[CHUNK:], qe[CHUNK:]], axis=0)
        kdt_s[n] = (k2s[n] * jnp.exp(g_lasts[n] - g_cols[n])).T.astype(BF16)

    onorm = onorm_ref[...]
    own_cols = ((lax.broadcasted_iota(jnp.int32, (PAIR, 2 * HEAD_DIM), 0) // CHUNK)
                == (lax.broadcasted_iota(jnp.int32, (PAIR, 2 * HEAD_DIM), 1) // HEAD_DIM))
    head1_lane = lax.broadcasted_iota(jnp.int32, (1, 2 * HEAD_DIM), 1) >= HEAD_DIM
    for c in range(GDN_BLOCK_CHUNKS):
        ns = [c * GDN_PAIRS + p for p in range(GDN_PAIRS)]
        sb = [s_ref[p].astype(BF16) for p in range(GDN_PAIRS)]
        r = [[dot(wq_s[n, hh], sb[p][:, hh * HEAD_DIM:(hh + 1) * HEAD_DIM]) for hh in range(2)]
             for p, n in enumerate(ns)]
        v_new = [jnp.concatenate([u_s[n][:CHUNK] - r[p][0][:CHUNK], u_s[n][CHUNK:] - r[p][1][:CHUNK]],
                                 axis=0).astype(BF16) for p, n in enumerate(ns)]
        o_intra = [dot(aqk_s[n], v_new[p]) for p, n in enumerate(ns)]
        for p, n in enumerate(ns):
            gl = gcol_ref[0, p, (c + 1) * CHUNK - 8:(c + 1) * CHUNK, :][7:8, :]
            keep = jnp.where(head1_lane, jnp.exp(gl[:, 1:2]), jnp.exp(gl[:, 0:1]))
            v_blk = jnp.where(own_cols, jnp.concatenate([v_new[p], v_new[p]], axis=1), jnp.zeros((), BF16))
            s_ref[p] = s_ref[p] * keep + dot(kdt_s[n], v_blk)
        for p, n in enumerate(ns):
            for hh in range(2):
                lo = (2 * p + hh) * HEAD_DIM
                rows = slice(c * CHUNK, (c + 1) * CHUNK)
                o = r[p][hh][CHUNK:] + o_intra[p][hh * CHUNK:(hh + 1) * CHUNK]
                o = _rms(o) * onorm * _silu(z_ref[0, rows, lo:lo + HEAD_DIM].astype(F32))
                o_ref[0, rows, lo:lo + HEAD_DIM] = o.astype(o_ref.dtype)


def gdn_mix(qkv, z, conv_w, beta, gcum, o_norm):
    b, lp, _ = qkv.shape
    nc = lp // CHUNK
    cb, pairs = GDN_BLOCK_CHUNKS, GDN_PAIRS
    assert nc % cb == 0 and GDN_QK_HEADS % pairs == 0
    nt, rb, ng = nc // cb, cb * CHUNK, GDN_QK_HEADS // pairs
    col = lambda t: t.reshape(b, lp, GDN_QK_HEADS, 2).transpose(0, 2, 1, 3)
    grow = gcum.reshape(b, nt, cb, CHUNK, GDN_QK_HEADS, 2).transpose(0, 4, 1, 2, 5, 3)
    grow = grow.reshape(b, GDN_QK_HEADS, nt, cb, PAIR)
    qw, vw = pairs * HEAD_DIM, 2 * pairs * HEAD_DIM
    k_off, v_off = GDN_QK_DIM // qw, 2 * GDN_QK_DIM // vw
    seq = lambda w, off: pl.BlockSpec((1, rb, w), lambda i, j, t, off=off: (i, t, off + j))
    halo = lambda w, off: pl.BlockSpec(
        (1, 16, w), lambda i, j, t, off=off: (i, jnp.maximum(t * (rb // 16) - 1, 0), off + j))
    cw = lambda w, off: pl.BlockSpec((GDN_CONV_K, w), lambda i, j, t, off=off: (0, off + j))
    gate = pl.BlockSpec((1, pairs, rb, 2), lambda i, j, t: (i, j, t, 0))
    n_units = cb * pairs
    return pl.pallas_call(
        _gdn_kernel,
        grid=(b, ng, nt),
        in_specs=[seq(qw, 0), seq(qw, k_off), seq(vw, v_off), seq(vw, 0),
                  halo(qw, 0), halo(qw, k_off), halo(vw, v_off),
                  cw(qw, 0), cw(qw, k_off), cw(vw, v_off),
                  gate, gate, pl.BlockSpec((1, pairs, 1, cb, PAIR), lambda i, j, t: (i, j, t, 0, 0)),
                  pl.BlockSpec((1, HEAD_DIM), lambda i, j, t: (0, 0))],
        out_specs=pl.BlockSpec((1, rb, vw), lambda i, j, t: (i, t, j)),
        out_shape=jax.ShapeDtypeStruct((b, lp, GDN_V_DIM), BF16),
        scratch_shapes=[pltpu.VMEM((pairs, HEAD_DIM, 2 * HEAD_DIM), F32),
                        pltpu.VMEM((n_units, PAIR, HEAD_DIM), F32),
                        pltpu.VMEM((n_units, 2, PAIR, HEAD_DIM), BF16),
                        pltpu.VMEM((n_units, PAIR, PAIR), BF16),
                        pltpu.VMEM((n_units, HEAD_DIM, PAIR), BF16)],
        compiler_params=_params("parallel", "parallel", "arbitrary"),
        name="gdn_mix",
    )(qkv, qkv, qkv, z, qkv, qkv, qkv, conv_w, conv_w, conv_w, col(beta), col(gcum), grow,
      o_norm.reshape(1, HEAD_DIM))


def _rope(x, cos, sin_lo, sin_hi):
    half = ROT_DIM // 2
    return x * cos + pltpu.roll(x, half, 1) * sin_hi + pltpu.roll(x, HEAD_DIM - half, 1) * sin_lo


def _attn_kernel(q_ref, k_ref, v_ref, cos_ref, slo_ref, shi_ref, lam_ref, subln_ref, o_ref,
                 kt_s, kh_s, m_s, l_s, acc_s, *, n_blocks, lambda_init):
    tb = ATT_BLOCK
    lam = lam_ref[...]
    lam_val = (jnp.exp(jnp.sum(lam[0:1] * lam[1:2], axis=-1, keepdims=True))
               - jnp.exp(jnp.sum(lam[2:3] * lam[3:4], axis=-1, keepdims=True)) + lambda_init)
    subln = subln_ref[...]
    maps = range(2)

    def roped(ref, r0, n, scale):
        x = ref[0, pl.ds(r0, n), :].astype(F32)
        tabs = (cos_ref[pl.ds(r0, n), :], slo_ref[pl.ds(r0, n), :], shi_ref[pl.ds(r0, n), :])
        return [_rope(x[:, m * HEAD_DIM:(m + 1) * HEAD_DIM], *tabs) * scale for m in maps]

    def dot(a, b):
        return jnp.dot(a, b, preferred_element_type=F32)

    def dot_t(a, b):
        return lax.dot_general(a, b, (((1,), (1,)), ((), ())), preferred_element_type=F32)

    kh = roped(k_ref, 0, CHUNK, 1.0)
    for m in maps:
        kh_s[:, m * HEAD_DIM:(m + 1) * HEAD_DIM] = kh[m].astype(BF16)
    for i in range(n_blocks):
        kb = roped(k_ref, CHUNK + i * tb, tb, 1.0)
        for m in maps:
            kt_s[i, m * HEAD_DIM:(m + 1) * HEAD_DIM, :] = kb[m].T.astype(BF16)

    def start(qm, nq, qrow0, causal_head):
        kpos = lax.broadcasted_iota(jnp.int32, (nq, CHUNK), 1)
        ok = kpos >= LEAD
        if causal_head:
            ok = ok & (kpos <= qrow0 + lax.broadcasted_iota(jnp.int32, (nq, CHUNK), 0))
        vh = v_ref[0, pl.ds(0, CHUNK), :]
        s = [jnp.where(ok, dot_t(qm[m], kh_s[:, m * HEAD_DIM:(m + 1) * HEAD_DIM]), MASK_VALUE) for m in maps]
        mx = [jnp.max(s[m], axis=-1, keepdims=True) for m in maps]
        p = [jnp.exp2(s[m] - mx[m]) for m in maps]
        lane0 = lax.broadcasted_iota(jnp.int32, (nq, LANES), 1) == 0
        for m in maps:
            m_s[m, pl.ds(0, nq), :] = jnp.broadcast_to(mx[m], (nq, LANES))
            l_s[m, pl.ds(0, nq), :] = jnp.where(lane0, jnp.sum(p[m], axis=-1, keepdims=True), 0.0)
            acc_s[m, pl.ds(0, nq), :] = dot(p[m].astype(BF16), vh)

    def lane_cols(x):
        return [x[:, c * LANES:(c + 1) * LANES] for c in range(x.shape[1] // LANES)]

    def update(qm, j, diag):
        vb = v_ref[0, pl.ds(pl.multiple_of(CHUNK + j * tb, CHUNK), tb), :]
        s = [dot(qm[m], kt_s[j, m * HEAD_DIM:(m + 1) * HEAD_DIM, :]) for m in maps]
        if diag:
            ok = (lax.broadcasted_iota(jnp.int32, (tb, tb), 1) <= lax.broadcasted_iota(jnp.int32, (tb, tb), 0))
            s = [jnp.where(ok, s[m], MASK_VALUE) for m in maps]
        sc = [lane_cols(s[m]) for m in maps]
        m_old = [m_s[m] for m in maps]
        m_new = [jnp.maximum(m_old[m], jnp.max(functools.reduce(jnp.maximum, sc[m]), axis=-1, keepdims=True))
                 for m in maps]
        pc = [[jnp.exp2(x - m_new[m]) for x in sc[m]] for m in maps]
        alpha = [jnp.exp2(m_old[m] - m_new[m]) for m in maps]
        pv = [dot(jnp.concatenate(pc[m], axis=1).astype(BF16), vb) for m in maps]
        for m in maps:
            m_s[m] = m_new[m]
            l_s[m] = alpha[m] * l_s[m] + functools.reduce(jnp.add, pc[m])
            acc_s[m] = jnp.concatenate([alpha[m]] * (DIFF_V_DIM // LANES), axis=1) * acc_s[m] + pv[m]

    def finish(r0, nq):
        l = [jnp.sum(l_s[m, pl.ds(0, nq), :], axis=-1, keepdims=True) for m in maps]
        o = acc_s[0, pl.ds(0, nq), :] / l[0] - lam_val * (acc_s[1, pl.ds(0, nq), :] / l[1])
        o_ref[0, pl.ds(r0, nq), :] = (_rms(o) * subln * (1.0 - lambda_init)).astype(o_ref.dtype)

    scale = HEAD_DIM ** -0.5 * math.log2(math.e)
    bf = lambda xs: [x.astype(BF16) for x in xs]
    start(bf(roped(q_ref, 0, CHUNK, scale)), CHUNK, 0, True)
    finish(0, CHUNK)

    def q_block(qi, carry):
        r0 = pl.multiple_of(CHUNK + qi * tb, CHUNK)
        qm = bf(roped(q_ref, r0, tb, scale))
        start(qm, tb, r0, False)

        def kv_block(j, c2):
            update(qm, j, False)
            return c2

        lax.fori_loop(0, qi, kv_block, 0)
        update(qm, qi, True)
        finish(r0, tb)
        return carry

    lax.fori_loop(0, n_blocks, q_block, 0)


def diff_attention(q, k, v, rope_tabs, lam, subln, lambda_init):
    b, lp, _ = q.shape
    n_blocks = (lp - CHUNK) // ATT_BLOCK
    assert CHUNK + n_blocks * ATT_BLOCK == lp
    head = pl.BlockSpec((1, lp, 2 * HEAD_DIM), lambda i, j: (i, 0, j))
    tab = pl.BlockSpec((lp, HEAD_DIM), lambda i, j: (0, 0))
    return pl.pallas_call(
        functools.partial(_attn_kernel, n_blocks=n_blocks, lambda_init=lambda_init),
        grid=(b, DIFF_HEADS),
        in_specs=[head, head, head, tab, tab, tab,
                  pl.BlockSpec((4, HEAD_DIM), lambda i, j: (0, 0)),
                  pl.BlockSpec((1, DIFF_V_DIM), lambda i, j: (0, 0))],
        out_specs=head,
        out_shape=jax.ShapeDtypeStruct((b, lp, DIFF_HEADS * DIFF_V_DIM), BF16),
        scratch_shapes=[pltpu.VMEM((n_blocks, 2 * HEAD_DIM, ATT_BLOCK), BF16),
                        pltpu.VMEM((CHUNK, 2 * HEAD_DIM), BF16),
                        pltpu.VMEM((2, ATT_BLOCK, LANES), F32),
                        pltpu.VMEM((2, ATT_BLOCK, LANES), F32),
                        pltpu.VMEM((2, ATT_BLOCK, DIFF_V_DIM), F32)],
        compiler_params=_params("parallel", "parallel"),
        name="diff_attention",
    )(q, k, v, *rope_tabs, lam, subln.reshape(1, DIFF_V_DIM))


def _rope_tables(lp):
    half = ROT_DIM // 2
    pos = jnp.maximum(jnp.arange(lp, dtype=F32) - LEAD, 0.0)
    inv_freq = ROPE_THETA ** (-jnp.arange(0, ROT_DIM, 2, dtype=F32) / ROT_DIM)
    ang = pos[:, None] * inv_freq[None, :]
    c, s = jnp.cos(ang), jnp.sin(ang)
    zeros = jnp.zeros((lp, HEAD_DIM - ROT_DIM), F32)
    cos = jnp.concatenate([c, c, jnp.ones_like(zeros)], axis=1)
    sin_lo = jnp.concatenate([-s, jnp.zeros((lp, half), F32), zeros], axis=1)
    sin_hi = jnp.concatenate([jnp.zeros((lp, half), F32), s, zeros], axis=1)
    return cos, sin_lo, sin_hi


def kernel(x, meta_tokens, norm_gains, mlp_w_up, mlp_w_down, gdn_w_in, gdn_conv_w, gdn_a_log, gdn_dt_bias,
           gdn_o_norm, gdn_w_out, kv_norm, w_kv, diff_w_q, diff_lambda, diff_subln, diff_w_o):
    b, seq, d = x.shape
    lp = LEAD + N_META + seq
    m = b * lp
    meta = jnp.broadcast_to(meta_tokens.astype(x.dtype)[None], (b, N_META, d))
    h = jnp.concatenate([jnp.zeros((b, LEAD, d), x.dtype), meta, x], axis=1).reshape(m, d)
    rope_tabs = _rope_tables(lp)
    wb = lambda w: w.astype(BF16)

    hn = rmsnorm_rows(h, norm_gains[0, 0])
    kv_k = kv_v = None
    for layer in range(DEPTH):
        if layer < N_A_LAYERS:
            w_in = gdn_w_in[layer]
            qkv = matmul(hn, wb(w_in[:, :GDN_CONV_DIM]), BF16)
            z = matmul(hn, wb(w_in[:, GDN_CONV_DIM:GDN_CONV_DIM + GDN_V_DIM]), BF16)
            w_ba = jnp.zeros((d, 2 * LANES), F32)
            w_ba = w_ba.at[:, :GDN_V_HEADS].set(w_in[:, GDN_CONV_DIM + GDN_V_DIM:GDN_CONV_DIM + GDN_V_DIM + GDN_V_HEADS])
            w_ba = w_ba.at[:, LANES:LANES + GDN_V_HEADS].set(w_in[:, GDN_CONV_DIM + GDN_V_DIM + GDN_V_HEADS:])
            ba = matmul(hn, wb(w_ba), F32)
            beta, gcum = gdn_gates(ba.reshape(b, lp, 2 * LANES), gdn_a_log[layer], gdn_dt_bias[layer])
            o = gdn_mix(qkv.reshape(b, lp, GDN_CONV_DIM), z.reshape(b, lp, GDN_V_DIM), gdn_conv_w[layer],
                        beta[..., :GDN_V_HEADS], gcum[..., :GDN_V_HEADS], gdn_o_norm[layer])
            mix = matmul(o.reshape(m, GDN_V_DIM), wb(gdn_w_out[layer]), F32)
        else:
            j = layer - N_A_LAYERS
            lambda_init = 0.8 - 0.6 * math.exp(-0.3 * layer)
            q = matmul(hn, wb(diff_w_q[j]), BF16)
            o = diff_attention(q.reshape(b, lp, DIFF_Q_DIM), kv_k, kv_v, rope_tabs, diff_lambda[j],
                               diff_subln[j], lambda_init)
            mix = matmul(o.reshape(m, DIFF_HEADS * DIFF_V_DIM), wb(diff_w_o[j]), F32)
        h, hn = resid_norm(h, mix, norm_gains[layer, 1], norm_gains[layer, 2:3])
        up = matmul(hn, wb(mlp_w_up[layer]), BF16, relu2=True)
        ff = matmul(up, wb(mlp_w_down[layer]), F32)
        if layer == N_A_LAYERS - 1:
            g_next = jnp.stack([norm_gains[layer + 1, 0], kv_norm])
            h, hn, hkv = resid_norm(h, ff, norm_gains[layer, 3], g_next)
            kv_k = matmul(hkv, wb(w_kv[:, :DIFF_Q_DIM]), BF16).reshape(b, lp, DIFF_Q_DIM)
            kv_v = matmul(hkv, wb(w_kv[:, DIFF_Q_DIM:]), BF16).reshape(b, lp, DIFF_HEADS * DIFF_V_DIM)
        elif layer + 1 < DEPTH:
            h, hn = resid_norm(h, ff, norm_gains[layer, 3], norm_gains[layer + 1, 0:1])
        else:
            h, = resid_norm(h, ff, norm_gains[layer, 3], None)
    return h.reshape(b, lp, d)[:, LEAD + N_META:]
```

```python
import functools
import math

import jax
import jax.numpy as jnp
from jax import lax
from jax.experimental import pallas as pl
from jax.experimental.pallas import tpu as pltpu

F32 = jnp.float32
BF16 = jnp.bfloat16

D_MODEL = 2048
DEPTH = 4
N_A_LAYERS = DEPTH // 2
N_META = 16
D_FF = 4 * D_MODEL
HEAD_DIM = 128
GDN_QK_HEADS = D_MODEL // HEAD_DIM
GDN_V_HEADS = 2 * GDN_QK_HEADS
GDN_QK_DIM = GDN_QK_HEADS * HEAD_DIM
GDN_V_DIM = GDN_V_HEADS * HEAD_DIM
GDN_CONV_DIM = 2 * GDN_QK_DIM + GDN_V_DIM
GDN_CONV_K = 4
CHUNK = 64
LEAD = CHUNK - N_META
PAIR = 2 * CHUNK
DIFF_HEADS = D_MODEL // 256
DIFF_V_DIM = 2 * HEAD_DIM
DIFF_Q_DIM = DIFF_HEADS * 2 * HEAD_DIM
ROT_DIM = HEAD_DIM // 4
ROPE_THETA = 500000.0
EPS = 1e-6
LANES = 128
GDN_PAIRS = 2
GDN_BLOCK_CHUNKS = 5
ATT_BLOCK = 512
MASK_VALUE = -1e30
VMEM_LIMIT_BYTES = 56 * 1024 * 1024


def _params(*sem):
    return pltpu.CompilerParams(dimension_semantics=sem, vmem_limit_bytes=VMEM_LIMIT_BYTES)


def _pick(n, cands):
    for c in cands:
        if n % c == 0:
            return c
    return n


def _rms(x):
    return x * lax.rsqrt(jnp.mean(x * x, axis=-1, keepdims=True) + EPS)


def _rmsnorm_kernel(x_ref, g_ref, o_ref):
    o_ref[...] = (_rms(x_ref[...]) * g_ref[...]).astype(o_ref.dtype)


def rmsnorm_rows(x, gain):
    m, d = x.shape
    tm = _pick(m, (640, 512, 320, 256, 128, 64, 32, 16, 8))
    return pl.pallas_call(
        _rmsnorm_kernel,
        grid=(m // tm,),
        in_specs=[pl.BlockSpec((tm, d), lambda i: (i, 0)), pl.BlockSpec((1, d), lambda i: (0, 0))],
        out_specs=pl.BlockSpec((tm, d), lambda i: (i, 0)),
        out_shape=jax.ShapeDtypeStruct((m, d), BF16),
        compiler_params=_params("parallel"),
        name="rmsnorm_rows",
    )(x, gain.reshape(1, d))


def _resid_norm_kernel(h_ref, y_ref, gp_ref, gn_ref, *out_refs, n_next):
    h_new = h_ref[...] + _rms(y_ref[...].astype(F32)) * gp_ref[...]
    out_refs[0][...] = h_new
    if n_next:
        hn = _rms(h_new)
        for j in range(n_next):
            out_refs[1 + j][...] = (hn * gn_ref[j:j + 1, :]).astype(BF16)


def resid_norm(h, y, g_post, g_next):
    m, d = h.shape
    n_next = 0 if g_next is None else g_next.shape[0]
    gn = jnp.zeros((1, d), F32) if g_next is None else g_next
    tm = _pick(m, (640, 512, 320, 256, 128, 64, 32, 16, 8))
    row = pl.BlockSpec((tm, d), lambda i: (i, 0))
    outs = pl.pallas_call(
        functools.partial(_resid_norm_kernel, n_next=n_next),
        grid=(m // tm,),
        in_specs=[row, row, pl.BlockSpec((1, d), lambda i: (0, 0)),
                  pl.BlockSpec(gn.shape, lambda i: (0, 0))],
        out_specs=[row] * (1 + n_next),
        out_shape=[jax.ShapeDtypeStruct((m, d), F32)] + [jax.ShapeDtypeStruct((m, d), BF16)] * n_next,
        compiler_params=_params("parallel"),
        name="resid_norm",
    )(h, y, g_post.reshape(1, d), gn)
    return outs


def _mm_kernel(x_ref, w_ref, o_ref, *acc, nk, relu2):
    def finish(r):
        if relu2:
            r = jnp.square(jnp.maximum(r, 0.0))
        o_ref[...] = r.astype(o_ref.dtype)

    part = jnp.dot(x_ref[...], w_ref[...], preferred_element_type=F32)
    if nk == 1:
        finish(part)
        return
    acc_ref, = acc
    k = pl.program_id(2)

    @pl.when(k == 0)
    def _():
        acc_ref[...] = part

    @pl.when(k > 0)
    def _():
        acc_ref[...] += part

    @pl.when(k == nk - 1)
    def _():
        finish(acc_ref[...])


def matmul(x, w, out_dtype, relu2=False):
    m, kdim = x.shape
    n = w.shape[1]
    tm = _pick(m, (1280, 1024, 640, 512, 320, 256, 128, 64, 32, 16, 8))
    tn = _pick(n, (1024, 512, 256, 128))
    tk = _pick(kdim, (2048,))
    nk = kdim // tk
    return pl.pallas_call(
        functools.partial(_mm_kernel, nk=nk, relu2=relu2),
        grid=(m // tm, n // tn, nk),
        in_specs=[pl.BlockSpec((tm, tk), lambda i, j, k: (i, k)),
                  pl.BlockSpec((tk, tn), lambda i, j, k: (k, j))],
        out_specs=pl.BlockSpec((tm, tn), lambda i, j, k: (i, j)),
        out_shape=jax.ShapeDtypeStruct((m, n), out_dtype),
        scratch_shapes=[pltpu.VMEM((tm, tn), F32)] if nk > 1 else [],
        compiler_params=_params("parallel", "parallel", "arbitrary"),
        name="matmul",
    )(x, w)


def _gates_kernel(ba_ref, alog_ref, dtb_ref, beta_ref, gcum_ref, *, n_chunks):
    row = lax.broadcasted_iota(jnp.int32, (CHUNK, CHUNK), 0)
    col = lax.broadcasted_iota(jnp.int32, (CHUNK, CHUNK), 1)
    tril = (row >= col).astype(F32)
    neg_rate = -jnp.exp(alog_ref[...])
    dtb = dtb_ref[...]

    def body(c, carry):
        r0 = pl.multiple_of(c * CHUNK, CHUNK)
        ba = ba_ref[0, pl.ds(r0, CHUNK), :]
        b, a = ba[:, :LANES], ba[:, LANES:]
        pos = r0 + lax.broadcasted_iota(jnp.int32, (CHUNK, LANES), 0)
        live = pos >= LEAD
        x = a + dtb
        softplus = jnp.maximum(x, 0.0) + jnp.log(1.0 + jnp.exp(-jnp.abs(x)))
        g = jnp.where(live, neg_rate * softplus, 0.0)
        beta_ref[0, pl.ds(r0, CHUNK), :] = jnp.where(live, 1.0 / (1.0 + jnp.exp(-b)), 0.0)
        gcum_ref[0, pl.ds(r0, CHUNK), :] = jnp.dot(tril, g, precision=lax.Precision.HIGHEST,
                                                   preferred_element_type=F32)
        return carry

    lax.fori_loop(0, n_chunks, body, 0)


def gdn_gates(ba, a_log, dt_bias):
    b, lp, _ = ba.shape
    pad = lambda t: jnp.zeros((1, LANES), F32).at[0, :GDN_V_HEADS].set(t.astype(F32))
    blk = pl.BlockSpec((1, lp, LANES), lambda i: (i, 0, 0))
    return pl.pallas_call(
        functools.partial(_gates_kernel, n_chunks=lp // CHUNK),
        grid=(b,),
        in_specs=[pl.BlockSpec((1, lp, 2 * LANES), lambda i: (i, 0, 0)),
                  pl.BlockSpec((1, LANES), lambda i: (0, 0)), pl.BlockSpec((1, LANES), lambda i: (0, 0))],
        out_specs=[blk, blk],
        out_shape=[jax.ShapeDtypeStruct((b, lp, LANES), F32)] * 2,
        compiler_params=_params("parallel"),
        name="gdn_gates",
    )(ba, pad(a_log), pad(dt_bias))


def _silu(x):
    return x / (1.0 + jnp.exp(-x))


def _gdn_step(q_ref, k_ref, v_ref, z_ref, qp_ref, kp_ref, vp_ref, wq_ref, wk_ref, wv_ref, brow_ref, grow_ref,
              onorm_ref, o_ref, s_ref, xf_s, wr, rd, first_block):
    cb, pairs = GDN_BLOCK_CHUNKS, GDN_PAIRS
    u_w, wq_w, aqk_w, kdt_w, keep_w = wr
    u_r, wq_r, aqk_r, kdt_r, keep_r = rd
    ri = lax.broadcasted_iota(jnp.int32, (PAIR, PAIR), 0)
    ci = lax.broadcasted_iota(jnp.int32, (PAIR, PAIR), 1)
    same_head = (ri // CHUNK) == (ci // CHUNK)
    incl = same_head & (ri >= ci)
    strict = same_head & (ri > ci)
    eye = (ri == ci).astype(F32)
    head1_lane = lax.broadcasted_iota(jnp.int32, (1, 2 * HEAD_DIM), 1) >= HEAD_DIM

    def sub_blocks(blk):
        return (((ri // blk) % 2 == 1) & ((ci // blk) % 2 == 0) & ((ri // (2 * blk)) == (ci // (2 * blk))))

    def dot(a, b):
        return jnp.dot(a, b, preferred_element_type=F32)

    def dot_t(a, b):
        return lax.dot_general(a, b, (((1,), (1,)), ((), ())), preferred_element_type=F32)

    def to_col(row):
        return jnp.sum(jnp.where(ri == ci, row, 0.0), axis=-1, keepdims=True)

    qw, vw = pairs * HEAD_DIM, 2 * pairs * HEAD_DIM
    seen_rows = jnp.where(first_block, 0.0, 1.0)
    for ref, halo_ref, lo, width in ((q_ref, qp_ref, 0, qw), (k_ref, kp_ref, qw, qw), (v_ref, vp_ref, 2 * qw, vw)):
        xf_s[0:8, lo:lo + width] = halo_ref[0].astype(F32)[8:, :] * seen_rows
        xf_s[8:, lo:lo + width] = ref[0].astype(F32)

    def conv_part(w_ref, c, lo, wlo, width):
        w = w_ref[:, wlo:wlo + width]
        y = None
        for s in range(GDN_CONV_K):
            r0 = 8 + c * CHUNK - s
            term = xf_s[r0:r0 + CHUNK, lo:lo + width] * w[GDN_CONV_K - 1 - s:GDN_CONV_K - s, :]
            y = term if y is None else y + term
        return _silu(y)

    units = [(c, p) for c in range(cb) for p in range(pairs)]
    st = dict(a_b=[], k2=[], q2=[], kb=[], v2=[], beta=[], g_col=[], g_last=[], ts=[], tbs=[], fts=[])

    def stage_a():
        for n, (c, p) in enumerate(units):
            q = conv_part(wq_ref, c, p * HEAD_DIM, p * HEAD_DIM, HEAD_DIM)
            k = conv_part(wk_ref, c, qw + p * HEAD_DIM, p * HEAD_DIM, HEAD_DIM)
            v = conv_part(wv_ref, c, 2 * qw + 2 * p * HEAD_DIM, 2 * p * HEAD_DIM, 2 * HEAD_DIM)
            q = q * lax.rsqrt(jnp.sum(q * q, axis=-1, keepdims=True) + EPS) * HEAD_DIM ** -0.5
            k = k * lax.rsqrt(jnp.sum(k * k, axis=-1, keepdims=True) + EPS)
            g_row = grow_ref[0, p, 0, c:c + 1, :]
            g_col = to_col(g_row)
            beta = to_col(brow_ref[0, p, 0, c:c + 1, :])
            gl = [g_row[:, (hh + 1) * CHUNK - 1:(hh + 1) * CHUNK] for hh in range(2)]
            g_last = jnp.where(ri[:, 0:1] < CHUNK, gl[0], gl[1])
            keep_w[n] = jnp.where(head1_lane, jnp.exp(gl[1]), jnp.exp(gl[0]))
            decay = jnp.exp(jnp.where(incl, g_col - g_row, -jnp.inf))
            k2 = jnp.concatenate([k, k], axis=0)
            q2 = jnp.concatenate([q, q], axis=0)
            kb = k2 * beta
            k2b = k2.astype(BF16)
            a_kk = jnp.where(strict, dot_t(kb.astype(BF16), k2b) * decay, 0.0)
            aqk_w[n] = (dot_t(q2.astype(BF16), k2b) * decay).astype(BF16)
            st["a_b"].append(a_kk.astype(BF16))
            st["ts"].append(eye - jnp.where(sub_blocks(1), a_kk, 0.0))
            st["k2"].append(k2)
            st["q2"].append(q2)
            st["kb"].append(kb)
            st["beta"].append(beta)
            st["v2"].append(jnp.concatenate([v[:, :HEAD_DIM], v[:, HEAD_DIM:]], axis=0))
            st["g_col"].append(g_col)
            st["g_last"].append(g_last)

    def level_ft(blk):
        mask = sub_blocks(blk).astype(BF16)
        st["tbs"] = [t.astype(BF16) for t in st["ts"]]
        st["fts"] = [dot(a * mask, tb).astype(BF16) for a, tb in zip(st["a_b"], st["tbs"])]

    def level_t(blk):
        st["ts"] = [t - dot(tb, ft) for t, tb, ft in zip(st["ts"], st["tbs"], st["fts"])]

    def stage_sol():
        for n in range(len(units)):
            e_g = jnp.exp(st["g_col"][n])
            rhs = jnp.concatenate([st["v2"][n] * st["beta"][n], st["kb"][n] * e_g], axis=1).astype(BF16)
            sol = dot(st["ts"][n].astype(BF16), rhs)
            u_w[n] = sol[:, :HEAD_DIM]
            w = sol[:, HEAD_DIM:].astype(BF16)
            qe = (st["q2"][n] * e_g).astype(BF16)
            wq_w[n, 0] = jnp.concatenate([w[:CHUNK], qe[:CHUNK]], axis=0)
            wq_w[n, 1] = jnp.concatenate([w[CHUNK:], qe[CHUNK:]], axis=0)
            kdt_w[n] = (st["k2"][n] * jnp.exp(st["g_last"][n] - st["g_col"][n])).T.astype(BF16)

    onorm = onorm_ref[...]
    own_cols = ((lax.broadcasted_iota(jnp.int32, (PAIR, 2 * HEAD_DIM), 0) // CHUNK)
                == (lax.broadcasted_iota(jnp.int32, (PAIR, 2 * HEAD_DIM), 1) // HEAD_DIM))
    seq = {}

    def read_state(c):
        sb = [s_ref[p].astype(BF16) for p in range(pairs)]
        seq["r"] = [[dot(wq_r[c * pairs + p, hh], sb[p][:, hh * HEAD_DIM:(hh + 1) * HEAD_DIM]) for hh in range(2)]
                    for p in range(pairs)]

    def write_state(c):
        r = seq["r"]
        for p in range(pairs):
            n = c * pairs + p
            u = u_r[n]
            v_new = jnp.concatenate([u[:CHUNK] - r[p][0][:CHUNK], u[CHUNK:] - r[p][1][:CHUNK]], axis=0).astype(BF16)
            o_intra = dot(aqk_r[n], v_new)
            v_blk = jnp.where(own_cols, jnp.concatenate([v_new, v_new], axis=1), jnp.zeros((), BF16))
            s_ref[p] = s_ref[p] * keep_r[n] + dot(kdt_r[n], v_blk)
            for hh in range(2):
                lo = (2 * p + hh) * HEAD_DIM
                rows = slice(c * CHUNK, (c + 1) * CHUNK)
                o = r[p][hh][CHUNK:] + o_intra[hh * CHUNK:(hh + 1) * CHUNK]
                o = _rms(o) * onorm * _silu(z_ref[0, rows, lo:lo + HEAD_DIM].astype(F32))
                o_ref[0, rows, lo:lo + HEAD_DIM] = o.astype(o_ref.dtype)

    prep = [stage_a]
    blk = 2
    while blk < CHUNK:
        prep += [functools.partial(level_ft, blk), functools.partial(level_t, blk)]
        blk *= 2
    prep.append(stage_sol)
    recur = [f for c in range(cb) for f in (functools.partial(read_state, c), functools.partial(write_state, c))]
    for i in range(max(len(prep), len(recur))):
        if i < len(recur):
            recur[i]()
        if i < len(prep):
            prep[i]()


def _gdn_kernel(*refs):
    ins, o_ref, (s_ref, xf_s), bufs = refs[:13], refs[13], refs[14:16], refs[16:]
    set0, set1 = bufs[:5], bufs[5:]
    t = pl.program_id(2)

    @pl.when(t == 0)
    def _():
        s_ref[...] = jnp.zeros_like(s_ref)
        for b in set1:
            b[...] = jnp.zeros_like(b)

    @pl.when(t % 2 == 0)
    def _():
        _gdn_step(*ins, o_ref, s_ref, xf_s, set0, set1, t == 0)

    @pl.when(t % 2 == 1)
    def _():
        _gdn_step(*ins, o_ref, s_ref, xf_s, set1, set0, t == 0)


def gdn_mix(qkv, z, conv_w, beta, gcum, o_norm):
    b, lp, _ = qkv.shape
    nc = lp // CHUNK
    cb, pairs = GDN_BLOCK_CHUNKS, GDN_PAIRS
    assert nc % cb == 0 and GDN_QK_HEADS % pairs == 0
    nt, rb, ng = nc // cb, cb * CHUNK, GDN_QK_HEADS // pairs
    rows = lambda t: (t.reshape(b, nt, cb, CHUNK, GDN_QK_HEADS, 2).transpose(0, 4, 1, 2, 5, 3)
                      .reshape(b, GDN_QK_HEADS, nt, cb, PAIR))
    qw, vw = pairs * HEAD_DIM, 2 * pairs * HEAD_DIM
    k_off, v_off = GDN_QK_DIM // qw, 2 * GDN_QK_DIM // vw
    cur = lambda t: jnp.minimum(t, nt - 1)
    prv = lambda t: jnp.maximum(t - 1, 0)
    seq = lambda w, off: pl.BlockSpec((1, rb, w), lambda i, j, t, off=off: (i, cur(t), off + j))
    halo = lambda w, off: pl.BlockSpec(
        (1, 16, w), lambda i, j, t, off=off: (i, jnp.maximum(cur(t) * (rb // 16) - 1, 0), off + j))
    cw = lambda w, off: pl.BlockSpec((GDN_CONV_K, w), lambda i, j, t, off=off: (0, off + j))
    gate = pl.BlockSpec((1, pairs, 1, cb, PAIR), lambda i, j, t: (i, j, cur(t), 0, 0))
    n_units = cb * pairs
    handover = [pltpu.VMEM((n_units, PAIR, HEAD_DIM), F32),
                pltpu.VMEM((n_units, 2, PAIR, HEAD_DIM), BF16),
                pltpu.VMEM((n_units, PAIR, PAIR), BF16),
                pltpu.VMEM((n_units, HEAD_DIM, PAIR), BF16),
                pltpu.VMEM((n_units, 1, 2 * HEAD_DIM), F32)]
    return pl.pallas_call(
        _gdn_kernel,
        grid=(b, ng, nt + 1),
        in_specs=[seq(qw, 0), seq(qw, k_off), seq(vw, v_off),
                  pl.BlockSpec((1, rb, vw), lambda i, j, t: (i, prv(t), j)),
                  halo(qw, 0), halo(qw, k_off), halo(vw, v_off),
                  cw(qw, 0), cw(qw, k_off), cw(vw, v_off),
                  gate, gate, pl.BlockSpec((1, HEAD_DIM), lambda i, j, t: (0, 0))],
        out_specs=pl.BlockSpec((1, rb, vw), lambda i, j, t: (i, prv(t), j)),
        out_shape=jax.ShapeDtypeStruct((b, lp, GDN_V_DIM), BF16),
        scratch_shapes=[pltpu.VMEM((pairs, HEAD_DIM, 2 * HEAD_DIM), F32),
                        pltpu.VMEM((rb + 8, 2 * qw + vw), F32)] + handover + handover,
        compiler_params=_params("parallel", "parallel", "arbitrary"),
        name="gdn_mix",
    )(qkv, qkv, qkv, z, qkv, qkv, qkv, conv_w, conv_w, conv_w, rows(beta), rows(gcum),
      o_norm.reshape(1, HEAD_DIM))


def _rope(x, cos, sin_lo, sin_hi):
    half = ROT_DIM // 2
    return x * cos + pltpu.roll(x, half, 1) * sin_hi + pltpu.roll(x, HEAD_DIM - half, 1) * sin_lo


def _attn_kernel(q_ref, k_ref, v_ref, cos_ref, slo_ref, shi_ref, lam_ref, subln_ref, o_ref,
                 kt_s, kh_s, m_s, l_s, acc_s, *, n_blocks, lambda_init):
    tb = ATT_BLOCK
    lam = lam_ref[...]
    lam_val = (jnp.exp(jnp.sum(lam[0:1] * lam[1:2], axis=-1, keepdims=True))
               - jnp.exp(jnp.sum(lam[2:3] * lam[3:4], axis=-1, keepdims=True)) + lambda_init)
    subln = subln_ref[...]
    maps = range(2)

    def roped(ref, r0, n, scale):
        x = ref[0, pl.ds(r0, n), :].astype(F32)
        tabs = (cos_ref[pl.ds(r0, n), :], slo_ref[pl.ds(r0, n), :], shi_ref[pl.ds(r0, n), :])
        return [_rope(x[:, m * HEAD_DIM:(m + 1) * HEAD_DIM], *tabs) * scale for m in maps]

    def dot(a, b):
        return jnp.dot(a, b, preferred_element_type=F32)

    def dot_t(a, b):
        return lax.dot_general(a, b, (((1,), (1,)), ((), ())), preferred_element_type=F32)

    kh = roped(k_ref, 0, CHUNK, 1.0)
    for m in maps:
        kh_s[:, m * HEAD_DIM:(m + 1) * HEAD_DIM] = kh[m].astype(BF16)
    for i in range(n_blocks):
        kb = roped(k_ref, CHUNK + i * tb, tb, 1.0)
        for m in maps:
            kt_s[i, m * HEAD_DIM:(m + 1) * HEAD_DIM, :] = kb[m].T.astype(BF16)

    def start(qm, nq, qrow0, causal_head):
        kpos = lax.broadcasted_iota(jnp.int32, (nq, CHUNK), 1)
        ok = kpos >= LEAD
        if causal_head:
            ok = ok & (kpos <= qrow0 + lax.broadcasted_iota(jnp.int32, (nq, CHUNK), 0))
        vh = v_ref[0, pl.ds(0, CHUNK), :]
        s = [jnp.where(ok, dot_t(qm[m], kh_s[:, m * HEAD_DIM:(m + 1) * HEAD_DIM]), MASK_VALUE) for m in maps]
        mx = [jnp.max(s[m], axis=-1, keepdims=True) for m in maps]
        p = [jnp.exp2(s[m] - mx[m]) for m in maps]
        lane0 = lax.broadcasted_iota(jnp.int32, (nq, LANES), 1) == 0
        for m in maps:
            m_s[m, pl.ds(0, nq), :] = jnp.broadcast_to(mx[m], (nq, LANES))
            l_s[m, pl.ds(0, nq), :] = jnp.where(lane0, jnp.sum(p[m], axis=-1, keepdims=True), 0.0)
            acc_s[m, pl.ds(0, nq), :] = dot(p[m].astype(BF16), vh)

    def lane_cols(x):
        return [x[:, c * LANES:(c + 1) * LANES] for c in range(x.shape[1] // LANES)]

    def scores(qm, j):
        return [dot(qm[m], kt_s[j, m * HEAD_DIM:(m + 1) * HEAD_DIM, :]) for m in maps]

    def update(s, j, diag):
        vb = v_ref[0, pl.ds(pl.multiple_of(CHUNK + j * tb, CHUNK), tb), :]
        if diag:
            ok = (lax.broadcasted_iota(jnp.int32, (tb, tb), 1) <= lax.broadcasted_iota(jnp.int32, (tb, tb), 0))
            s = [jnp.where(ok, s[m], MASK_VALUE) for m in maps]
        sc = [lane_cols(s[m]) for m in maps]
        m_old = [m_s[m] for m in maps]
        m_new = [jnp.maximum(m_old[m], jnp.max(functools.reduce(jnp.maximum, sc[m]), axis=-1, keepdims=True))
                 for m in maps]
        pc = [[jnp.exp2(x - m_new[m]) for x in sc[m]] for m in maps]
        alpha = [jnp.exp2(m_old[m] - m_new[m]) for m in maps]
        pv = [dot(jnp.concatenate(pc[m], axis=1).astype(BF16), vb) for m in maps]
        for m in maps:
            m_s[m] = m_new[m]
            l_s[m] = alpha[m] * l_s[m] + functools.reduce(jnp.add, pc[m])
            acc_s[m] = jnp.concatenate([alpha[m]] * (DIFF_V_DIM // LANES), axis=1) * acc_s[m] + pv[m]

    def finish(r0, nq):
        l = [jnp.sum(l_s[m, pl.ds(0, nq), :], axis=-1, keepdims=True) for m in maps]
        o = acc_s[0, pl.ds(0, nq), :] / l[0] - lam_val * (acc_s[1, pl.ds(0, nq), :] / l[1])
        o_ref[0, pl.ds(r0, nq), :] = (_rms(o) * subln * (1.0 - lambda_init)).astype(o_ref.dtype)

    scale = HEAD_DIM ** -0.5 * math.log2(math.e)
    bf = lambda xs: [x.astype(BF16) for x in xs]
    start(bf(roped(q_ref, 0, CHUNK, scale)), CHUNK, 0, True)
    finish(0, CHUNK)

    def q_block(qi, carry):
        r0 = pl.multiple_of(CHUNK + qi * tb, CHUNK)
        qm = bf(roped(q_ref, r0, tb, scale))
        start(qm, tb, r0, False)

        def kv_block(j, c2):
            update(scores(qm, j), j, False)
            return c2

        lax.fori_loop(0, qi, kv_block, 0)
        update(scores(qm, qi), qi, True)
        finish(r0, tb)
        return carry

    lax.fori_loop(0, n_blocks, q_block, 0)


def diff_attention(q, k, v, rope_tabs, lam, subln, lambda_init):
    b, lp, _ = q.shape
    n_blocks = (lp - CHUNK) // ATT_BLOCK
    assert CHUNK + n_blocks * ATT_BLOCK == lp
    head = pl.BlockSpec((1, lp, 2 * HEAD_DIM), lambda i, j: (i, 0, j))
    tab = pl.BlockSpec((lp, HEAD_DIM), lambda i, j: (0, 0))
    return pl.pallas_call(
        functools.partial(_attn_kernel, n_blocks=n_blocks, lambda_init=lambda_init),
        grid=(b, DIFF_HEADS),
        in_specs=[head, head, head, tab, tab, tab,
                  pl.BlockSpec((4, HEAD_DIM), lambda i, j: (0, 0)),
                  pl.BlockSpec((1, DIFF_V_DIM), lambda i, j: (0, 0))],
        out_specs=head,
        out_shape=jax.ShapeDtypeStruct((b, lp, DIFF_HEADS * DIFF_V_DIM), BF16),
        scratch_shapes=[pltpu.VMEM((n_blocks, 2 * HEAD_DIM, ATT_BLOCK), BF16),
                        pltpu.VMEM((CHUNK, 2 * HEAD_DIM), BF16),
                        pltpu.VMEM((2, ATT_BLOCK, LANES), F32),
                        pltpu.VMEM((2, ATT_BLOCK, LANES), F32),
                        pltpu.VMEM((2, ATT_BLOCK, DIFF_V_DIM), F32)],
        compiler_params=_params("parallel", "parallel"),
        name="diff_attention",
    )(q, k, v, *rope_tabs, lam, subln.reshape(1, DIFF_V_DIM))


def _rope_tables(lp):
    half = ROT_DIM // 2
    pos = jnp.maximum(jnp.arange(lp, dtype=F32) - LEAD, 0.0)
    inv_freq = ROPE_THETA ** (-jnp.arange(0, ROT_DIM, 2, dtype=F32) / ROT_DIM)
    ang = pos[:, None] * inv_freq[None, :]
    c, s = jnp.cos(ang), jnp.sin(ang)
    zeros = jnp.zeros((lp, HEAD_DIM - ROT_DIM), F32)
    cos = jnp.concatenate([c, c, jnp.ones_like(zeros)], axis=1)
    sin_lo = jnp.concatenate([-s, jnp.zeros((lp, half), F32), zeros], axis=1)
    sin_hi = jnp.concatenate([jnp.zeros((lp, half), F32), s, zeros], axis=1)
    return cos, sin_lo, sin_hi


def kernel(x, meta_tokens, norm_gains, mlp_w_up, mlp_w_down, gdn_w_in, gdn_conv_w, gdn_a_log, gdn_dt_bias,
           gdn_o_norm, gdn_w_out, kv_norm, w_kv, diff_w_q, diff_lambda, diff_subln, diff_w_o):
    b, seq, d = x.shape
    lp = LEAD + N_META + seq
    m = b * lp
    meta = jnp.broadcast_to(meta_tokens.astype(x.dtype)[None], (b, N_META, d))
    h = jnp.concatenate([jnp.zeros((b, LEAD, d), x.dtype), meta, x], axis=1).reshape(m, d)
    rope_tabs = _rope_tables(lp)
    wb = lambda w: w.astype(BF16)

    hn = rmsnorm_rows(h, norm_gains[0, 0])
    kv_k = kv_v = None
    for layer in range(DEPTH):
        if layer < N_A_LAYERS:
            w_in = gdn_w_in[layer]
            qkv = matmul(hn, wb(w_in[:, :GDN_CONV_DIM]), BF16)
            z = matmul(hn, wb(w_in[:, GDN_CONV_DIM:GDN_CONV_DIM + GDN_V_DIM]), BF16)
            w_ba = jnp.zeros((d, 2 * LANES), F32)
            w_ba = w_ba.at[:, :GDN_V_HEADS].set(w_in[:, GDN_CONV_DIM + GDN_V_DIM:GDN_CONV_DIM + GDN_V_DIM + GDN_V_HEADS])
            w_ba = w_ba.at[:, LANES:LANES + GDN_V_HEADS].set(w_in[:, GDN_CONV_DIM + GDN_V_DIM + GDN_V_HEADS:])
            ba = matmul(hn, wb(w_ba), F32)
            beta, gcum = gdn_gates(ba.reshape(b, lp, 2 * LANES), gdn_a_log[layer], gdn_dt_bias[layer])
            o = gdn_mix(qkv.reshape(b, lp, GDN_CONV_DIM), z.reshape(b, lp, GDN_V_DIM), gdn_conv_w[layer],
                        beta[..., :GDN_V_HEADS], gcum[..., :GDN_V_HEADS], gdn_o_norm[layer])
            mix = matmul(o.reshape(m, GDN_V_DIM), wb(gdn_w_out[layer]), BF16)
        else:
            j = layer - N_A_LAYERS
            lambda_init = 0.8 - 0.6 * math.exp(-0.3 * layer)
            q = matmul(hn, wb(diff_w_q[j]), BF16)
            o = diff_attention(q.reshape(b, lp, DIFF_Q_DIM), kv_k, kv_v, rope_tabs, diff_lambda[j],
                               diff_subln[j], lambda_init)
            mix = matmul(o.reshape(m, DIFF_HEADS * DIFF_V_DIM), wb(diff_w_o[j]), BF16)
        h, hn = resid_norm(h, mix, norm_gains[layer, 1], norm_gains[layer, 2:3])
        up = matmul(hn, wb(mlp_w_up[layer]), BF16, relu2=True)
        ff = matmul(up, wb(mlp_w_down[layer]), BF16)
        if layer == N_A_LAYERS - 1:
            g_next = jnp.stack([norm_gains[layer + 1, 0], kv_norm])
            h, hn, hkv = resid_norm(h, ff, norm_gains[layer, 3], g_next)
            kv_k = matmul(hkv, wb(w_kv[:, :DIFF_Q_DIM]), BF16).reshape(b, lp, DIFF_Q_DIM)
            kv_v = matmul(hkv, wb(w_kv[:, DIFF_Q_DIM:]), BF16).reshape(b, lp, DIFF_HEADS * DIFF_V_DIM)
        elif layer + 1 < DEPTH:
            h, hn = resid_norm(h, ff, norm_gains[layer, 3], norm_gains[layer + 1, 0:1])
        else:
            h, = resid_norm(h, ff, norm_gains[layer, 3], None)
    return h.reshape(b, lp, d)[:, LEAD + N_META:]
```

```python
import functools
import math

import jax
import jax.numpy as jnp
import numpy as np
from jax import lax
from jax.experimental import pallas as pl
from jax.experimental.pallas import tpu as pltpu

F32 = jnp.float32
BF16 = jnp.bfloat16

D_MODEL = 2048
DEPTH = 4
N_A_LAYERS = DEPTH // 2
N_META = 16
D_FF = 4 * D_MODEL
HEAD_DIM = 128
GDN_QK_HEADS = D_MODEL // HEAD_DIM
GDN_V_HEADS = 2 * GDN_QK_HEADS
GDN_QK_DIM = GDN_QK_HEADS * HEAD_DIM
GDN_V_DIM = GDN_V_HEADS * HEAD_DIM
GDN_CONV_DIM = 2 * GDN_QK_DIM + GDN_V_DIM
GDN_CONV_K = 4
CHUNK = 64
LEAD = CHUNK - N_META
PAIR = 2 * CHUNK
DIFF_HEADS = D_MODEL // 256
DIFF_V_DIM = 2 * HEAD_DIM
DIFF_Q_DIM = DIFF_HEADS * 2 * HEAD_DIM
ROT_DIM = HEAD_DIM // 4
ROPE_THETA = 500000.0
EPS = 1e-6
LANES = 128
GDN_PAIRS = 2
GDN_BLOCK_CHUNKS = 5
ATT_BLOCK = 512
MASK_VALUE = -1e30
VMEM_LIMIT_BYTES = 56 * 1024 * 1024


def _params(*sem):
    return pltpu.CompilerParams(dimension_semantics=sem, vmem_limit_bytes=VMEM_LIMIT_BYTES)


def _pick(n, cands):
    for c in cands:
        if n % c == 0:
            return c
    return n


def _rms(x):
    return x * lax.rsqrt(jnp.mean(x * x, axis=-1, keepdims=True) + EPS)


def _rmsnorm_kernel(x_ref, g_ref, o_ref):
    o_ref[...] = (_rms(x_ref[...]) * g_ref[...]).astype(o_ref.dtype)


def rmsnorm_rows(x, gain):
    m, d = x.shape
    tm = _pick(m, (640, 512, 320, 256, 128, 64, 32, 16, 8))
    return pl.pallas_call(
        _rmsnorm_kernel,
        grid=(m // tm,),
        in_specs=[pl.BlockSpec((tm, d), lambda i: (i, 0)), pl.BlockSpec((1, d), lambda i: (0, 0))],
        out_specs=pl.BlockSpec((tm, d), lambda i: (i, 0)),
        out_shape=jax.ShapeDtypeStruct((m, d), BF16),
        compiler_params=_params("parallel"),
        name="rmsnorm_rows",
    )(x, gain.reshape(1, d))


def _resid_norm_kernel(h_ref, y_ref, gp_ref, gn_ref, *out_refs, n_next):
    h_new = h_ref[...] + _rms(y_ref[...].astype(F32)) * gp_ref[...]
    out_refs[0][...] = h_new
    if n_next:
        hn = _rms(h_new)
        for j in range(n_next):
            out_refs[1 + j][...] = (hn * gn_ref[j:j + 1, :]).astype(BF16)


def resid_norm(h, y, g_post, g_next):
    m, d = h.shape
    n_next = 0 if g_next is None else g_next.shape[0]
    gn = jnp.zeros((1, d), F32) if g_next is None else g_next
    tm = _pick(m, (640, 512, 320, 256, 128, 64, 32, 16, 8))
    row = pl.BlockSpec((tm, d), lambda i: (i, 0))
    outs = pl.pallas_call(
        functools.partial(_resid_norm_kernel, n_next=n_next),
        grid=(m // tm,),
        in_specs=[row, row, pl.BlockSpec((1, d), lambda i: (0, 0)),
                  pl.BlockSpec(gn.shape, lambda i: (0, 0))],
        out_specs=[row] * (1 + n_next),
        out_shape=[jax.ShapeDtypeStruct((m, d), F32)] + [jax.ShapeDtypeStruct((m, d), BF16)] * n_next,
        compiler_params=_params("parallel"),
        name="resid_norm",
    )(h, y, g_post.reshape(1, d), gn)
    return outs


def _mm_kernel(x_ref, w_ref, o_ref, *acc, nk, relu2):
    def finish(r):
        if relu2:
            r = jnp.square(jnp.maximum(r, 0.0))
        o_ref[...] = r.astype(o_ref.dtype)

    part = jnp.dot(x_ref[...], w_ref[...], preferred_element_type=F32)
    if nk == 1:
        finish(part)
        return
    acc_ref, = acc
    k = pl.program_id(2)

    @pl.when(k == 0)
    def _():
        acc_ref[...] = part

    @pl.when(k > 0)
    def _():
        acc_ref[...] += part

    @pl.when(k == nk - 1)
    def _():
        finish(acc_ref[...])


def matmul(x, w, out_dtype, relu2=False, cols=None):
    m, kdim = x.shape
    lo, hi = cols or (0, w.shape[1])
    n = hi - lo
    tm = _pick(m, (1280, 1024, 640, 512, 320, 256, 128, 64, 32, 16, 8))
    tn = _pick(math.gcd(n, lo) if lo else n, (1024, 512, 256, 128))
    tk = _pick(kdim, (2048,))
    nk = kdim // tk
    assert lo % tn == 0 and n % tn == 0
    first = lo // tn
    return pl.pallas_call(
        functools.partial(_mm_kernel, nk=nk, relu2=relu2),
        grid=(m // tm, n // tn, nk),
        in_specs=[pl.BlockSpec((tm, tk), lambda i, j, k: (i, k)),
                  pl.BlockSpec((tk, tn), lambda i, j, k: (k, first + j))],
        out_specs=pl.BlockSpec((tm, tn), lambda i, j, k: (i, j)),
        out_shape=jax.ShapeDtypeStruct((m, n), out_dtype),
        scratch_shapes=[pltpu.VMEM((tm, tn), F32)] if nk > 1 else [],
        compiler_params=_params("parallel", "parallel", "arbitrary"),
        name="matmul",
    )(x, w)


def _gates_kernel(ba_ref, alog_ref, dtb_ref, beta_ref, gcum_ref, *, n_chunks):
    row = lax.broadcasted_iota(jnp.int32, (CHUNK, CHUNK), 0)
    col = lax.broadcasted_iota(jnp.int32, (CHUNK, CHUNK), 1)
    tril = (row >= col).astype(F32)
    neg_rate = -jnp.exp(alog_ref[...])
    dtb = dtb_ref[...]

    def body(c, carry):
        r0 = pl.multiple_of(c * CHUNK, CHUNK)
        ba = ba_ref[0, pl.ds(r0, CHUNK), :]
        b, a = ba[:, :LANES], ba[:, LANES:]
        pos = r0 + lax.broadcasted_iota(jnp.int32, (CHUNK, LANES), 0)
        live = pos >= LEAD
        x = a + dtb
        softplus = jnp.maximum(x, 0.0) + jnp.log(1.0 + jnp.exp(-jnp.abs(x)))
        g = jnp.where(live, neg_rate * softplus, 0.0)
        beta_ref[0, pl.ds(r0, CHUNK), :] = jnp.where(live, 1.0 / (1.0 + jnp.exp(-b)), 0.0)
        gcum_ref[0, pl.ds(r0, CHUNK), :] = jnp.dot(tril, g, precision=lax.Precision.HIGHEST,
                                                   preferred_element_type=F32)
        return carry

    lax.fori_loop(0, n_chunks, body, 0)


def gdn_gates(ba, a_log, dt_bias):
    b, lp, _ = ba.shape
    pad = lambda t: jnp.zeros((1, LANES), F32).at[0, :GDN_V_HEADS].set(t.astype(F32))
    blk = pl.BlockSpec((1, lp, LANES), lambda i: (i, 0, 0))
    return pl.pallas_call(
        functools.partial(_gates_kernel, n_chunks=lp // CHUNK),
        grid=(b,),
        in_specs=[pl.BlockSpec((1, lp, 2 * LANES), lambda i: (i, 0, 0)),
                  pl.BlockSpec((1, LANES), lambda i: (0, 0)), pl.BlockSpec((1, LANES), lambda i: (0, 0))],
        out_specs=[blk, blk],
        out_shape=[jax.ShapeDtypeStruct((b, lp, LANES), F32)] * 2,
        compiler_params=_params("parallel"),
        name="gdn_gates",
    )(ba, pad(a_log), pad(dt_bias))


def _silu(x):
    return x / (1.0 + jnp.exp(-x))


def _gdn_masks():
    ri, ci = np.indices((PAIR, PAIR))
    same_head = (ri // CHUNK) == (ci // CHUNK)

    def sub_blocks(blk):
        return ((ri // blk) % 2 == 1) & ((ci // blk) % 2 == 0) & ((ri // (2 * blk)) == (ci // (2 * blk)))

    levels = []
    blk = 2
    while blk < CHUNK:
        levels.append(sub_blocks(blk))
        blk *= 2
    f32_masks = np.stack([ri == ci, ri != ci, sub_blocks(1)]).astype(np.float32)
    causal = np.where(same_head & (ri >= ci), 0.0, -np.inf).astype(np.float32)
    rr, cc = np.indices((PAIR, 2 * HEAD_DIM))
    own_cols = (rr // CHUNK) == (cc // HEAD_DIM)
    return (jnp.asarray(np.concatenate([f32_masks, causal[None]])), jnp.asarray(np.stack(levels), BF16),
            jnp.asarray(own_cols, BF16))


def _gdn_step(q_ref, k_ref, v_ref, z_ref, qp_ref, kp_ref, vp_ref, wq_ref, wk_ref, wv_ref, brow_ref, grow_ref,
              onorm_ref, mf_ref, ml_ref, own_ref, o_ref, s_ref, xf_s, wr, rd, first_block):
    cb, pairs = GDN_BLOCK_CHUNKS, GDN_PAIRS
    u_w, wq_w, aqk_w, kdt_w, keep_w = wr
    u_r, wq_r, aqk_r, kdt_r, keep_r = rd
    eye, off_diag, sub1, causal = mf_ref[0], mf_ref[1], mf_ref[2], mf_ref[3]
    head1_lane = lax.broadcasted_iota(jnp.int32, (1, 2 * HEAD_DIM), 1) >= HEAD_DIM
    head1_row = lax.broadcasted_iota(jnp.int32, (PAIR, 1), 0) >= CHUNK

    def dot(a, b):
        return jnp.dot(a, b, preferred_element_type=F32)

    def dot_t(a, b):
        return lax.dot_general(a, b, (((1,), (1,)), ((), ())), preferred_element_type=F32)

    def to_col(row):
        return jnp.sum(row * eye, axis=-1, keepdims=True)

    qw, vw = pairs * HEAD_DIM, 2 * pairs * HEAD_DIM
    seen_rows = jnp.where(first_block, 0.0, 1.0)
    for ref, halo_ref, lo, width in ((q_ref, qp_ref, 0, qw), (k_ref, kp_ref, qw, qw), (v_ref, vp_ref, 2 * qw, vw)):
        xf_s[0:8, lo:lo + width] = halo_ref[0].astype(F32)[8:, :] * seen_rows
        xf_s[8:, lo:lo + width] = ref[0].astype(F32)

    def conv_part(w_ref, c, lo, wlo, width):
        w = w_ref[:, wlo:wlo + width]
        y = None
        for s in range(GDN_CONV_K):
            r0 = 8 + c * CHUNK - s
            term = xf_s[r0:r0 + CHUNK, lo:lo + width] * w[GDN_CONV_K - 1 - s:GDN_CONV_K - s, :]
            y = term if y is None else y + term
        return _silu(y)

    units = [(c, p) for c in range(cb) for p in range(pairs)]
    st = dict(a_b=[], k2=[], q2=[], kb=[], v2=[], beta=[], g_col=[], g_last=[], ts=[], tbs=[], fts=[])

    def stage_a():
        for n, (c, p) in enumerate(units):
            q = conv_part(wq_ref, c, p * HEAD_DIM, p * HEAD_DIM, HEAD_DIM)
            k = conv_part(wk_ref, c, qw + p * HEAD_DIM, p * HEAD_DIM, HEAD_DIM)
            v = conv_part(wv_ref, c, 2 * qw + 2 * p * HEAD_DIM, 2 * p * HEAD_DIM, 2 * HEAD_DIM)
            q = q * lax.rsqrt(jnp.sum(q * q, axis=-1, keepdims=True) + EPS) * HEAD_DIM ** -0.5
            k = k * lax.rsqrt(jnp.sum(k * k, axis=-1, keepdims=True) + EPS)
            g_row = grow_ref[0, p, 0, c:c + 1, :]
            g_col = to_col(g_row)
            beta = to_col(brow_ref[0, p, 0, c:c + 1, :])
            gl = [g_row[:, (hh + 1) * CHUNK - 1:(hh + 1) * CHUNK] for hh in range(2)]
            g_last = jnp.where(head1_row, gl[1], gl[0])
            keep_w[n] = jnp.where(head1_lane, jnp.exp(gl[1]), jnp.exp(gl[0]))
            decay = jnp.exp((g_col - g_row) + causal)
            k2 = jnp.concatenate([k, k], axis=0)
            q2 = jnp.concatenate([q, q], axis=0)
            kb = k2 * beta
            k2b = k2.astype(BF16)
            a_kk = dot_t(kb.astype(BF16), k2b) * decay * off_diag
            aqk_w[n] = (dot_t(q2.astype(BF16), k2b) * decay).astype(BF16)
            st["a_b"].append(a_kk.astype(BF16))
            st["ts"].append(eye - a_kk * sub1)
            st["k2"].append(k2)
            st["q2"].append(q2)
            st["kb"].append(kb)
            st["beta"].append(beta)
            st["v2"].append(jnp.concatenate([v[:, :HEAD_DIM], v[:, HEAD_DIM:]], axis=0))
            st["g_col"].append(g_col)
            st["g_last"].append(g_last)

    def level_ft(level):
        mask = ml_ref[level]
        st["tbs"] = [t.astype(BF16) for t in st["ts"]]
        st["fts"] = [dot(a * mask, tb).astype(BF16) for a, tb in zip(st["a_b"], st["tbs"])]

    def level_t(blk):
        st["ts"] = [t - dot(tb, ft) for t, tb, ft in zip(st["ts"], st["tbs"], st["fts"])]

    def stage_sol():
        for n in range(len(units)):
            e_g = jnp.exp(st["g_col"][n])
            rhs = jnp.concatenate([st["v2"][n] * st["beta"][n], st["kb"][n] * e_g], axis=1).astype(BF16)
            sol = dot(st["ts"][n].astype(BF16), rhs)
            u_w[n] = sol[:, :HEAD_DIM]
            w = sol[:, HEAD_DIM:].astype(BF16)
            qe = (st["q2"][n] * e_g).astype(BF16)
            wq_w[n, 0] = jnp.concatenate([w[:CHUNK], qe[:CHUNK]], axis=0)
            wq_w[n, 1] = jnp.concatenate([w[CHUNK:], qe[CHUNK:]], axis=0)
            kdt_w[n] = (st["k2"][n] * jnp.exp(st["g_last"][n] - st["g_col"][n])).T.astype(BF16)

    onorm = onorm_ref[...]
    seq = {}

    def read_state(c):
        sb = [s_ref[p].astype(BF16) for p in range(pairs)]
        seq["r"] = [[dot(wq_r[c * pairs + p, hh], sb[p][:, hh * HEAD_DIM:(hh + 1) * HEAD_DIM]) for hh in range(2)]
                    for p in range(pairs)]

    def write_state(c):
        r = seq["r"]
        for p in range(pairs):
            n = c * pairs + p
            u = u_r[n]
            v_new = jnp.concatenate([u[:CHUNK] - r[p][0][:CHUNK], u[CHUNK:] - r[p][1][:CHUNK]], axis=0).astype(BF16)
            o_intra = dot(aqk_r[n], v_new)
            v_blk = jnp.concatenate([v_new, v_new], axis=1) * own_ref[...]
            s_ref[p] = s_ref[p] * keep_r[n] + dot(kdt_r[n], v_blk)
            for hh in range(2):
                lo = (2 * p + hh) * HEAD_DIM
                rows = slice(c * CHUNK, (c + 1) * CHUNK)
                o = r[p][hh][CHUNK:] + o_intra[hh * CHUNK:(hh + 1) * CHUNK]
                o = _rms(o) * onorm * _silu(z_ref[0, rows, lo:lo + HEAD_DIM].astype(F32))
                o_ref[0, rows, lo:lo + HEAD_DIM] = o.astype(o_ref.dtype)

    prep = [stage_a]
    for level in range(ml_ref.shape[0]):
        prep += [functools.partial(level_ft, level), functools.partial(level_t, level)]
    prep.append(stage_sol)
    recur = [f for c in range(cb) for f in (functools.partial(read_state, c), functools.partial(write_state, c))]
    for i in range(max(len(prep), len(recur))):
        if i < len(recur):
            recur[i]()
        if i < len(prep):
            prep[i]()


def _gdn_kernel(*refs):
    ins, o_ref, (s_ref, xf_s), bufs = refs[:16], refs[16], refs[17:19], refs[19:]
    set0, set1 = bufs[:5], bufs[5:]
    t = pl.program_id(2)

    @pl.when(t == 0)
    def _():
        s_ref[...] = jnp.zeros_like(s_ref)
        for b in set1:
            b[...] = jnp.zeros_like(b)

    @pl.when(t % 2 == 0)
    def _():
        _gdn_step(*ins, o_ref, s_ref, xf_s, set0, set1, t == 0)

    @pl.when(t % 2 == 1)
    def _():
        _gdn_step(*ins, o_ref, s_ref, xf_s, set1, set0, t == 0)


def gdn_mix(qkv, z, conv_w, beta, gcum, o_norm):
    b, lp, _ = qkv.shape
    nc = lp // CHUNK
    cb, pairs = GDN_BLOCK_CHUNKS, GDN_PAIRS
    assert nc % cb == 0 and GDN_QK_HEADS % pairs == 0
    nt, rb, ng = nc // cb, cb * CHUNK, GDN_QK_HEADS // pairs
    rows = lambda t: (t.reshape(b, nt, cb, CHUNK, GDN_QK_HEADS, 2).transpose(0, 4, 1, 2, 5, 3)
                      .reshape(b, GDN_QK_HEADS, nt, cb, PAIR))
    qw, vw = pairs * HEAD_DIM, 2 * pairs * HEAD_DIM
    k_off, v_off = GDN_QK_DIM // qw, 2 * GDN_QK_DIM // vw
    cur = lambda t: jnp.minimum(t, nt - 1)
    prv = lambda t: jnp.maximum(t - 1, 0)
    seq = lambda w, off: pl.BlockSpec((1, rb, w), lambda i, j, t, off=off: (i, cur(t), off + j))
    halo = lambda w, off: pl.BlockSpec(
        (1, 16, w), lambda i, j, t, off=off: (i, jnp.maximum(cur(t) * (rb // 16) - 1, 0), off + j))
    cw = lambda w, off: pl.BlockSpec((GDN_CONV_K, w), lambda i, j, t, off=off: (0, off + j))
    gate = pl.BlockSpec((1, pairs, 1, cb, PAIR), lambda i, j, t: (i, j, cur(t), 0, 0))
    n_units = cb * pairs
    masks = _gdn_masks()
    handover = [pltpu.VMEM((n_units, PAIR, HEAD_DIM), F32),
                pltpu.VMEM((n_units, 2, PAIR, HEAD_DIM), BF16),
                pltpu.VMEM((n_units, PAIR, PAIR), BF16),
                pltpu.VMEM((n_units, HEAD_DIM, PAIR), BF16),
                pltpu.VMEM((n_units, 1, 2 * HEAD_DIM), F32)]
    return pl.pallas_call(
        _gdn_kernel,
        grid=(b, ng, nt + 1),
        in_specs=[seq(qw, 0), seq(qw, k_off), seq(vw, v_off),
                  pl.BlockSpec((1, rb, vw), lambda i, j, t: (i, prv(t), j)),
                  halo(qw, 0), halo(qw, k_off), halo(vw, v_off),
                  cw(qw, 0), cw(qw, k_off), cw(vw, v_off),
                  gate, gate, pl.BlockSpec((1, HEAD_DIM), lambda i, j, t: (0, 0))]
                 + [pl.BlockSpec(mk.shape, lambda i, j, t, nd=mk.ndim: (0,) * nd) for mk in masks],
        out_specs=pl.BlockSpec((1, rb, vw), lambda i, j, t: (i, prv(t), j)),
        out_shape=jax.ShapeDtypeStruct((b, lp, GDN_V_DIM), BF16),
        scratch_shapes=[pltpu.VMEM((pairs, HEAD_DIM, 2 * HEAD_DIM), F32),
                        pltpu.VMEM((rb + 8, 2 * qw + vw), F32)] + handover + handover,
        compiler_params=_params("parallel", "parallel", "arbitrary"),
        name="gdn_mix",
    )(qkv, qkv, qkv, z, qkv, qkv, qkv, conv_w, conv_w, conv_w, rows(beta), rows(gcum),
      o_norm.reshape(1, HEAD_DIM), *masks)


def _rope(x, cos, sin_lo, sin_hi):
    half = ROT_DIM // 2
    return x * cos + pltpu.roll(x, half, 1) * sin_hi + pltpu.roll(x, HEAD_DIM - half, 1) * sin_lo


def _attn_kernel(q_ref, k_ref, v_ref, cos_ref, slo_ref, shi_ref, lam_ref, subln_ref, o_ref,
                 kt_s, kh_s, m_s, l_s, acc_s, sc_s, *, n_blocks, lambda_init):
    tb = ATT_BLOCK
    lam = lam_ref[...]
    lam_val = (jnp.exp(jnp.sum(lam[0:1] * lam[1:2], axis=-1, keepdims=True))
               - jnp.exp(jnp.sum(lam[2:3] * lam[3:4], axis=-1, keepdims=True)) + lambda_init)
    subln = subln_ref[...]
    maps = range(2)

    def roped(ref, r0, n, scale):
        x = ref[0, pl.ds(r0, n), :].astype(F32)
        tabs = (cos_ref[pl.ds(r0, n), :], slo_ref[pl.ds(r0, n), :], shi_ref[pl.ds(r0, n), :])
        return [_rope(x[:, m * HEAD_DIM:(m + 1) * HEAD_DIM], *tabs) * scale for m in maps]

    def dot(a, b):
        return jnp.dot(a, b, preferred_element_type=F32)

    def dot_t(a, b):
        return lax.dot_general(a, b, (((1,), (1,)), ((), ())), preferred_element_type=F32)

    kh = roped(k_ref, 0, CHUNK, 1.0)
    for m in maps:
        kh_s[:, m * HEAD_DIM:(m + 1) * HEAD_DIM] = kh[m].astype(BF16)
    for i in range(n_blocks):
        kb = roped(k_ref, CHUNK + i * tb, tb, 1.0)
        for m in maps:
            kt_s[i, m * HEAD_DIM:(m + 1) * HEAD_DIM, :] = kb[m].T.astype(BF16)

    def start(qm, nq, qrow0, causal_head):
        kpos = lax.broadcasted_iota(jnp.int32, (nq, CHUNK), 1)
        ok = kpos >= LEAD
        if causal_head:
            ok = ok & (kpos <= qrow0 + lax.broadcasted_iota(jnp.int32, (nq, CHUNK), 0))
        vh = v_ref[0, pl.ds(0, CHUNK), :]
        s = [jnp.where(ok, dot_t(qm[m], kh_s[:, m * HEAD_DIM:(m + 1) * HEAD_DIM]), MASK_VALUE) for m in maps]
        mx = [jnp.max(s[m], axis=-1, keepdims=True) for m in maps]
        p = [jnp.exp2(s[m] - mx[m]) for m in maps]
        lane0 = lax.broadcasted_iota(jnp.int32, (nq, LANES), 1) == 0
        for m in maps:
            m_s[m, pl.ds(0, nq), :] = jnp.broadcast_to(mx[m], (nq, LANES))
            l_s[m, pl.ds(0, nq), :] = jnp.where(lane0, jnp.sum(p[m], axis=-1, keepdims=True), 0.0)
            acc_s[m, pl.ds(0, nq), :] = dot(p[m].astype(BF16), vh)

    def lane_cols(x):
        return [x[:, c * LANES:(c + 1) * LANES] for c in range(x.shape[1] // LANES)]

    def scores(qm, j):
        return [dot(qm[m], kt_s[j, m * HEAD_DIM:(m + 1) * HEAD_DIM, :]) for m in maps]

    def update(s, j, diag):
        vb = v_ref[0, pl.ds(pl.multiple_of(CHUNK + j * tb, CHUNK), tb), :]
        if diag:
            ok = (lax.broadcasted_iota(jnp.int32, (tb, tb), 1) <= lax.broadcasted_iota(jnp.int32, (tb, tb), 0))
            s = [jnp.where(ok, s[m], MASK_VALUE) for m in maps]
        sc = [lane_cols(s[m]) for m in maps]
        m_old = [m_s[m] for m in maps]
        m_new = [jnp.maximum(m_old[m], jnp.max(functools.reduce(jnp.maximum, sc[m]), axis=-1, keepdims=True))
                 for m in maps]
        pc = [[jnp.exp2(x - m_new[m]) for x in sc[m]] for m in maps]
        alpha = [jnp.exp2(m_old[m] - m_new[m]) for m in maps]
        pv = [dot(jnp.concatenate(pc[m], axis=1).astype(BF16), vb) for m in maps]
        for m in maps:
            m_s[m] = m_new[m]
            l_s[m] = alpha[m] * l_s[m] + functools.reduce(jnp.add, pc[m])
            acc_s[m] = jnp.concatenate([alpha[m]] * (DIFF_V_DIM // LANES), axis=1) * acc_s[m] + pv[m]

    def finish(r0, nq):
        l = [jnp.sum(l_s[m, pl.ds(0, nq), :], axis=-1, keepdims=True) for m in maps]
        o = acc_s[0, pl.ds(0, nq), :] / l[0] - lam_val * (acc_s[1, pl.ds(0, nq), :] / l[1])
        o_ref[0, pl.ds(r0, nq), :] = (_rms(o) * subln * (1.0 - lambda_init)).astype(o_ref.dtype)

    scale = HEAD_DIM ** -0.5 * math.log2(math.e)
    bf = lambda xs: [x.astype(BF16) for x in xs]
    start(bf(roped(q_ref, 0, CHUNK, scale)), CHUNK, 0, True)
    finish(0, CHUNK)

    def q_block(qi, carry):
        r0 = pl.multiple_of(CHUNK + qi * tb, CHUNK)
        qm = bf(roped(q_ref, r0, tb, scale))
        start(qm, tb, r0, False)

        def produce(slot, j):
            for m, s in enumerate(scores(qm, j)):
                sc_s[slot, m] = s

        def consume(slot, j, diag):
            update([sc_s[slot, m] for m in maps], j, diag)

        produce(0, 0)

        def kv_pair(jj, c2):
            j = 2 * jj
            produce(1, j + 1)
            consume(0, j, False)
            produce(0, j + 2)
            consume(1, j + 1, False)
            return c2

        lax.fori_loop(0, qi // 2, kv_pair, 0)

        @pl.when(qi % 2 == 0)
        def _():
            consume(0, qi, True)

        @pl.when(qi % 2 == 1)
        def _():
            produce(1, qi)
            consume(0, qi - 1, False)
            consume(1, qi, True)

        finish(r0, tb)
        return carry

    lax.fori_loop(0, n_blocks, q_block, 0)


def diff_attention(q, k, v, rope_tabs, lam, subln, lambda_init):
    b, lp, _ = q.shape
    n_blocks = (lp - CHUNK) // ATT_BLOCK
    assert CHUNK + n_blocks * ATT_BLOCK == lp
    head = pl.BlockSpec((1, lp, 2 * HEAD_DIM), lambda i, j: (i, 0, j))
    tab = pl.BlockSpec((lp, HEAD_DIM), lambda i, j: (0, 0))
    return pl.pallas_call(
        functools.partial(_attn_kernel, n_blocks=n_blocks, lambda_init=lambda_init),
        grid=(b, DIFF_HEADS),
        in_specs=[head, head, head, tab, tab, tab,
                  pl.BlockSpec((4, HEAD_DIM), lambda i, j: (0, 0)),
                  pl.BlockSpec((1, DIFF_V_DIM), lambda i, j: (0, 0))],
        out_specs=head,
        out_shape=jax.ShapeDtypeStruct((b, lp, DIFF_HEADS * DIFF_V_DIM), BF16),
        scratch_shapes=[pltpu.VMEM((n_blocks, 2 * HEAD_DIM, ATT_BLOCK), BF16),
                        pltpu.VMEM((CHUNK, 2 * HEAD_DIM), BF16),
                        pltpu.VMEM((2, ATT_BLOCK, LANES), F32),
                        pltpu.VMEM((2, ATT_BLOCK, LANES), F32),
                        pltpu.VMEM((2, ATT_BLOCK, DIFF_V_DIM), F32),
                        pltpu.VMEM((2, 2, ATT_BLOCK, ATT_BLOCK), F32)],
        compiler_params=_params("parallel", "parallel"),
        name="diff_attention",
    )(q, k, v, *rope_tabs, lam, subln.reshape(1, DIFF_V_DIM))


def _rope_tables(lp):
    half = ROT_DIM // 2
    pos = jnp.maximum(jnp.arange(lp, dtype=F32) - LEAD, 0.0)
    inv_freq = ROPE_THETA ** (-jnp.arange(0, ROT_DIM, 2, dtype=F32) / ROT_DIM)
    ang = pos[:, None] * inv_freq[None, :]
    c, s = jnp.cos(ang), jnp.sin(ang)
    zeros = jnp.zeros((lp, HEAD_DIM - ROT_DIM), F32)
    cos = jnp.concatenate([c, c, jnp.ones_like(zeros)], axis=1)
    sin_lo = jnp.concatenate([-s, jnp.zeros((lp, half), F32), zeros], axis=1)
    sin_hi = jnp.concatenate([jnp.zeros((lp, half), F32), s, zeros], axis=1)
    return cos, sin_lo, sin_hi


def kernel(x, meta_tokens, norm_gains, mlp_w_up, mlp_w_down, gdn_w_in, gdn_conv_w, gdn_a_log, gdn_dt_bias,
           gdn_o_norm, gdn_w_out, kv_norm, w_kv, diff_w_q, diff_lambda, diff_subln, diff_w_o):
    b, seq, d = x.shape
    lp = LEAD + N_META + seq
    m = b * lp
    meta = jnp.broadcast_to(meta_tokens.astype(x.dtype)[None], (b, N_META, d))
    h = jnp.concatenate([jnp.zeros((b, LEAD, d), x.dtype), meta, x], axis=1).reshape(m, d)
    rope_tabs = _rope_tables(lp)
    wb = lambda w: w.astype(BF16)

    hn = rmsnorm_rows(h, norm_gains[0, 0])
    kv_k = kv_v = None
    for layer in range(DEPTH):
        if layer < N_A_LAYERS:
            w_in = gdn_w_in[layer]
            w_in_b = wb(w_in)
            qkv = matmul(hn, w_in_b, BF16, cols=(0, GDN_CONV_DIM))
            z = matmul(hn, w_in_b, BF16, cols=(GDN_CONV_DIM, GDN_CONV_DIM + GDN_V_DIM))
            w_ba = jnp.zeros((d, 2 * LANES), F32)
            w_ba = w_ba.at[:, :GDN_V_HEADS].set(w_in[:, GDN_CONV_DIM + GDN_V_DIM:GDN_CONV_DIM + GDN_V_DIM + GDN_V_HEADS])
            w_ba = w_ba.at[:, LANES:LANES + GDN_V_HEADS].set(w_in[:, GDN_CONV_DIM + GDN_V_DIM + GDN_V_HEADS:])
            ba = matmul(hn, wb(w_ba), F32)
            beta, gcum = gdn_gates(ba.reshape(b, lp, 2 * LANES), gdn_a_log[layer], gdn_dt_bias[layer])
            o = gdn_mix(qkv.reshape(b, lp, GDN_CONV_DIM), z.reshape(b, lp, GDN_V_DIM), gdn_conv_w[layer],
                        beta[..., :GDN_V_HEADS], gcum[..., :GDN_V_HEADS], gdn_o_norm[layer])
            mix = matmul(o.reshape(m, GDN_V_DIM), wb(gdn_w_out[layer]), BF16)
        else:
            j = layer - N_A_LAYERS
            lambda_init = 0.8 - 0.6 * math.exp(-0.3 * layer)
            q = matmul(hn, wb(diff_w_q[j]), BF16)
            o = diff_attention(q.reshape(b, lp, DIFF_Q_DIM), kv_k, kv_v, rope_tabs, diff_lambda[j],
                               diff_subln[j], lambda_init)
            mix = matmul(o.reshape(m, DIFF_HEADS * DIFF_V_DIM), wb(diff_w_o[j]), BF16)
        h, hn = resid_norm(h, mix, norm_gains[layer, 1], norm_gains[layer, 2:3])
        up = matmul(hn, wb(mlp_w_up[layer]), BF16, relu2=True)
        ff = matmul(up, wb(mlp_w_down[layer]), BF16)
        if layer == N_A_LAYERS - 1:
            g_next = jnp.stack([norm_gains[layer + 1, 0], kv_norm])
            h, hn, hkv = resid_norm(h, ff, norm_gains[layer, 3], g_next)
            w_kv_b = wb(w_kv)
            kv_k = matmul(hkv, w_kv_b, BF16, cols=(0, DIFF_Q_DIM)).reshape(b, lp, DIFF_Q_DIM)
            kv_v = matmul(hkv, w_kv_b, BF16, cols=(DIFF_Q_DIM, w_kv.shape[1])).reshape(b, lp, DIFF_HEADS * DIFF_V_DIM)
        elif layer + 1 < DEPTH:
            h, hn = resid_norm(h, ff, norm_gains[layer, 3], norm_gains[layer + 1, 0:1])
        else:
            h, = resid_norm(h, ff, norm_gains[layer, 3], None)
    return h.reshape(b, lp, d)[:, LEAD + N_META:]
```

```python
import functools
import math

import jax
import jax.numpy as jnp
import numpy as np
from jax import lax
from jax.experimental import pallas as pl
from jax.experimental.pallas import tpu as pltpu

F32 = jnp.float32
BF16 = jnp.bfloat16

D_MODEL = 2048
DEPTH = 4
N_A_LAYERS = DEPTH // 2
N_META = 16
D_FF = 4 * D_MODEL
HEAD_DIM = 128
GDN_QK_HEADS = D_MODEL // HEAD_DIM
GDN_V_HEADS = 2 * GDN_QK_HEADS
GDN_QK_DIM = GDN_QK_HEADS * HEAD_DIM
GDN_V_DIM = GDN_V_HEADS * HEAD_DIM
GDN_CONV_DIM = 2 * GDN_QK_DIM + GDN_V_DIM
GDN_CONV_K = 4
CHUNK = 64
LEAD = CHUNK - N_META
PAIR = 2 * CHUNK
DIFF_HEADS = D_MODEL // 256
DIFF_V_DIM = 2 * HEAD_DIM
DIFF_Q_DIM = DIFF_HEADS * 2 * HEAD_DIM
ROT_DIM = HEAD_DIM // 4
ROPE_THETA = 500000.0
EPS = 1e-6
LANES = 128
GDN_PAIRS = 2
GDN_BLOCK_CHUNKS = 5
ATT_BLOCK = 512
MASK_VALUE = -1e30
VMEM_LIMIT_BYTES = 56 * 1024 * 1024


def _params(*sem):
    return pltpu.CompilerParams(dimension_semantics=sem, vmem_limit_bytes=VMEM_LIMIT_BYTES)


def _pick(n, cands):
    for c in cands:
        if n % c == 0:
            return c
    return n


def _rms(x):
    return x * lax.rsqrt(jnp.mean(x * x, axis=-1, keepdims=True) + EPS)


def _rmsnorm_kernel(x_ref, g_ref, o_ref):
    o_ref[...] = (_rms(x_ref[...]) * g_ref[...]).astype(o_ref.dtype)


def rmsnorm_rows(x, gain):
    m, d = x.shape
    tm = _pick(m, (640, 512, 320, 256, 128, 64, 32, 16, 8))
    return pl.pallas_call(
        _rmsnorm_kernel,
        grid=(m // tm,),
        in_specs=[pl.BlockSpec((tm, d), lambda i: (i, 0)), pl.BlockSpec((1, d), lambda i: (0, 0))],
        out_specs=pl.BlockSpec((tm, d), lambda i: (i, 0)),
        out_shape=jax.ShapeDtypeStruct((m, d), BF16),
        compiler_params=_params("parallel"),
        name="rmsnorm_rows",
    )(x, gain.reshape(1, d))


def _resid_norm_kernel(h_ref, y_ref, gp_ref, gn_ref, *out_refs, n_next):
    h_new = h_ref[...] + _rms(y_ref[...].astype(F32)) * gp_ref[...]
    out_refs[0][...] = h_new
    if n_next:
        hn = _rms(h_new)
        for j in range(n_next):
            out_refs[1 + j][...] = (hn * gn_ref[j:j + 1, :]).astype(BF16)


def resid_norm(h, y, g_post, g_next):
    m, d = h.shape
    n_next = 0 if g_next is None else g_next.shape[0]
    gn = jnp.zeros((1, d), F32) if g_next is None else g_next
    tm = _pick(m, (640, 512, 320, 256, 128, 64, 32, 16, 8))
    row = pl.BlockSpec((tm, d), lambda i: (i, 0))
    outs = pl.pallas_call(
        functools.partial(_resid_norm_kernel, n_next=n_next),
        grid=(m // tm,),
        in_specs=[row, row, pl.BlockSpec((1, d), lambda i: (0, 0)),
                  pl.BlockSpec(gn.shape, lambda i: (0, 0))],
        out_specs=[row] * (1 + n_next),
        out_shape=[jax.ShapeDtypeStruct((m, d), F32)] + [jax.ShapeDtypeStruct((m, d), BF16)] * n_next,
        compiler_params=_params("parallel"),
        name="resid_norm",
    )(h, y, g_post.reshape(1, d), gn)
    return outs


def _mm_kernel(x_ref, w_ref, o_ref, *acc, nk, relu2):
    def finish(r):
        if relu2:
            r = jnp.square(jnp.maximum(r, 0.0))
        o_ref[...] = r.astype(o_ref.dtype)

    part = jnp.dot(x_ref[...], w_ref[...], preferred_element_type=F32)
    if nk == 1:
        finish(part)
        return
    acc_ref, = acc
    k = pl.program_id(2)

    @pl.when(k == 0)
    def _():
        acc_ref[...] = part

    @pl.when(k > 0)
    def _():
        acc_ref[...] += part

    @pl.when(k == nk - 1)
    def _():
        finish(acc_ref[...])


def matmul(x, w, out_dtype, relu2=False, cols=None, layer=None):
    m, kdim = x.shape
    lo, hi = cols or (0, w.shape[-1])
    n = hi - lo
    tm = _pick(m, (1280, 1024, 640, 512, 320, 256, 128, 64, 32, 16, 8))
    tn = _pick(math.gcd(n, lo) if lo else n, (1024, 512, 256, 128))
    tk = kdim if kdim <= 4096 else 2048
    nk = kdim // tk
    assert lo % tn == 0 and n % tn == 0 and (w.ndim == 3) == (layer is not None)
    first = lo // tn
    if layer is None:
        w_spec = pl.BlockSpec((tk, tn), lambda i, j, k: (k, first + j))
    else:
        w_spec = pl.BlockSpec((None, tk, tn), lambda i, j, k: (layer, k, first + j))
    return pl.pallas_call(
        functools.partial(_mm_kernel, nk=nk, relu2=relu2),
        grid=(m // tm, n // tn, nk),
        in_specs=[pl.BlockSpec((tm, tk), lambda i, j, k: (i, k)), w_spec],
        out_specs=pl.BlockSpec((tm, tn), lambda i, j, k: (i, j)),
        out_shape=jax.ShapeDtypeStruct((m, n), out_dtype),
        scratch_shapes=[pltpu.VMEM((tm, tn), F32)] if nk > 1 else [],
        compiler_params=_params("parallel", "parallel", "arbitrary"),
        name="matmul",
    )(x, w)


def _gates_kernel(ba_ref, alog_ref, dtb_ref, beta_ref, gcum_ref, *, n_chunks):
    row = lax.broadcasted_iota(jnp.int32, (CHUNK, CHUNK), 0)
    col = lax.broadcasted_iota(jnp.int32, (CHUNK, CHUNK), 1)
    tril = (row >= col).astype(F32)
    neg_rate = -jnp.exp(alog_ref[...])
    dtb = dtb_ref[...]

    def body(c, carry):
        r0 = pl.multiple_of(c * CHUNK, CHUNK)
        ba = ba_ref[0, pl.ds(r0, CHUNK), :]
        b, a = ba[:, :LANES], ba[:, LANES:]
        pos = r0 + lax.broadcasted_iota(jnp.int32, (CHUNK, LANES), 0)
        live = pos >= LEAD
        x = a + dtb
        softplus = jnp.maximum(x, 0.0) + jnp.log(1.0 + jnp.exp(-jnp.abs(x)))
        g = jnp.where(live, neg_rate * softplus, 0.0)
        beta_ref[0, pl.ds(r0, CHUNK), :] = jnp.where(live, 1.0 / (1.0 + jnp.exp(-b)), 0.0)
        gcum_ref[0, pl.ds(r0, CHUNK), :] = jnp.dot(tril, g, precision=lax.Precision.HIGHEST,
                                                   preferred_element_type=F32)
        return carry

    lax.fori_loop(0, n_chunks, body, 0)


def gdn_gates(ba, a_log, dt_bias):
    b, lp, _ = ba.shape
    pad = lambda t: jnp.zeros((1, LANES), F32).at[0, :GDN_V_HEADS].set(t.astype(F32))
    blk = pl.BlockSpec((1, lp, LANES), lambda i: (i, 0, 0))
    return pl.pallas_call(
        functools.partial(_gates_kernel, n_chunks=lp // CHUNK),
        grid=(b,),
        in_specs=[pl.BlockSpec((1, lp, 2 * LANES), lambda i: (i, 0, 0)),
                  pl.BlockSpec((1, LANES), lambda i: (0, 0)), pl.BlockSpec((1, LANES), lambda i: (0, 0))],
        out_specs=[blk, blk],
        out_shape=[jax.ShapeDtypeStruct((b, lp, LANES), F32)] * 2,
        compiler_params=_params("parallel"),
        name="gdn_gates",
    )(ba, pad(a_log), pad(dt_bias))


def _silu(x):
    return x / (1.0 + jnp.exp(-x))


def _gdn_masks():
    ri, ci = np.indices((PAIR, PAIR))
    same_head = (ri // CHUNK) == (ci // CHUNK)

    def sub_blocks(blk):
        return ((ri // blk) % 2 == 1) & ((ci // blk) % 2 == 0) & ((ri // (2 * blk)) == (ci // (2 * blk)))

    levels = []
    blk = 2
    while blk < CHUNK:
        levels.append(sub_blocks(blk))
        blk *= 2
    f32_masks = np.stack([ri == ci, ri != ci, sub_blocks(1)]).astype(np.float32)
    causal = np.where(same_head & (ri >= ci), 0.0, -np.inf).astype(np.float32)
    rr, cc = np.indices((PAIR, 2 * HEAD_DIM))
    own_cols = (rr // CHUNK) == (cc // HEAD_DIM)
    return (jnp.asarray(np.concatenate([f32_masks, causal[None]])), jnp.asarray(np.stack(levels), BF16),
            jnp.asarray(own_cols, BF16))


def _gdn_step(q_ref, k_ref, v_ref, z_ref, qp_ref, kp_ref, vp_ref, wq_ref, wk_ref, wv_ref, brow_ref, grow_ref,
              onorm_ref, mf_ref, ml_ref, own_ref, o_ref, s_ref, xf_s, wr, rd, first_block,
              do_prep=True, do_recur=True):
    cb, pairs = GDN_BLOCK_CHUNKS, GDN_PAIRS
    u_w, wq_w, aqk_w, kdt_w, keep_w = wr
    u_r, wq_r, aqk_r, kdt_r, keep_r = rd
    eye, off_diag, sub1, causal = mf_ref[0], mf_ref[1], mf_ref[2], mf_ref[3]
    head1_lane = lax.broadcasted_iota(jnp.int32, (1, 2 * HEAD_DIM), 1) >= HEAD_DIM
    head1_row = lax.broadcasted_iota(jnp.int32, (PAIR, 1), 0) >= CHUNK

    def dot(a, b):
        return jnp.dot(a, b, preferred_element_type=F32)

    def dot_t(a, b):
        return lax.dot_general(a, b, (((1,), (1,)), ((), ())), preferred_element_type=F32)

    def to_col(row):
        return jnp.sum(row * eye, axis=-1, keepdims=True)

    qw, vw = pairs * HEAD_DIM, 2 * pairs * HEAD_DIM
    seen_rows = jnp.where(first_block, 0.0, 1.0)
    for ref, halo_ref, lo, width in ((q_ref, qp_ref, 0, qw), (k_ref, kp_ref, qw, qw), (v_ref, vp_ref, 2 * qw, vw)):
        if do_prep:
            xf_s[0:8, lo:lo + width] = halo_ref[0].astype(F32)[8:, :] * seen_rows
            xf_s[8:, lo:lo + width] = ref[0].astype(F32)

    def conv_part(w_ref, c, lo, wlo, width):
        w = w_ref[:, wlo:wlo + width]
        y = None
        for s in range(GDN_CONV_K):
            r0 = 8 + c * CHUNK - s
            term = xf_s[r0:r0 + CHUNK, lo:lo + width] * w[GDN_CONV_K - 1 - s:GDN_CONV_K - s, :]
            y = term if y is None else y + term
        return _silu(y)

    units = [(c, p) for c in range(cb) for p in range(pairs)]
    st = dict(a_b=[], k2=[], q2=[], kb=[], v2=[], beta=[], g_col=[], g_last=[], ts=[], tbs=[], fts=[])

    def stage_a():
        for n, (c, p) in enumerate(units):
            q = conv_part(wq_ref, c, p * HEAD_DIM, p * HEAD_DIM, HEAD_DIM)
            k = conv_part(wk_ref, c, qw + p * HEAD_DIM, p * HEAD_DIM, HEAD_DIM)
            v = conv_part(wv_ref, c, 2 * qw + 2 * p * HEAD_DIM, 2 * p * HEAD_DIM, 2 * HEAD_DIM)
            q = q * lax.rsqrt(jnp.sum(q * q, axis=-1, keepdims=True) + EPS) * HEAD_DIM ** -0.5
            k = k * lax.rsqrt(jnp.sum(k * k, axis=-1, keepdims=True) + EPS)
            g_row = grow_ref[0, p, 0, c:c + 1, :]
            g_col = to_col(g_row)
            beta = to_col(brow_ref[0, p, 0, c:c + 1, :])
            gl = [g_row[:, (hh + 1) * CHUNK - 1:(hh + 1) * CHUNK] for hh in range(2)]
            g_last = jnp.where(head1_row, gl[1], gl[0])
            keep_w[n] = jnp.where(head1_lane, jnp.exp(gl[1]), jnp.exp(gl[0]))
            decay = jnp.exp((g_col - g_row) + causal)
            k2 = jnp.concatenate([k, k], axis=0)
            q2 = jnp.concatenate([q, q], axis=0)
            kb = k2 * beta
            k2b = k2.astype(BF16)
            a_kk = dot_t(kb.astype(BF16), k2b) * decay * off_diag
            aqk_w[n] = (dot_t(q2.astype(BF16), k2b) * decay).astype(BF16)
            st["a_b"].append(a_kk.astype(BF16))
            st["ts"].append(eye - a_kk * sub1)
            st["k2"].append(k2)
            st["q2"].append(q2)
            st["kb"].append(kb)
            st["beta"].append(beta)
            st["v2"].append(jnp.concatenate([v[:, :HEAD_DIM], v[:, HEAD_DIM:]], axis=0))
            st["g_col"].append(g_col)
            st["g_last"].append(g_last)

    def level_ft(level):
        mask = ml_ref[level]
        st["tbs"] = [t.astype(BF16) for t in st["ts"]]
        st["fts"] = [dot(a * mask, tb).astype(BF16) for a, tb in zip(st["a_b"], st["tbs"])]

    def level_t(blk):
        st["ts"] = [t - dot(tb, ft) for t, tb, ft in zip(st["ts"], st["tbs"], st["fts"])]

    def stage_sol():
        for n in range(len(units)):
            e_g = jnp.exp(st["g_col"][n])
            rhs = jnp.concatenate([st["v2"][n] * st["beta"][n], st["kb"][n] * e_g], axis=1).astype(BF16)
            sol = dot(st["ts"][n].astype(BF16), rhs)
            u_w[n] = sol[:, :HEAD_DIM]
            w = sol[:, HEAD_DIM:].astype(BF16)
            qe = (st["q2"][n] * e_g).astype(BF16)
            wq_w[n, 0] = jnp.concatenate([w[:CHUNK], qe[:CHUNK]], axis=0)
            wq_w[n, 1] = jnp.concatenate([w[CHUNK:], qe[CHUNK:]], axis=0)
            kdt_w[n] = (st["k2"][n] * jnp.exp(st["g_last"][n] - st["g_col"][n])).T.astype(BF16)

    onorm = onorm_ref[...]
    seq = {}

    def read_state(c):
        sb = [s_ref[p].astype(BF16) for p in range(pairs)]
        seq["r"] = [[dot(wq_r[c * pairs + p, hh], sb[p][:, hh * HEAD_DIM:(hh + 1) * HEAD_DIM]) for hh in range(2)]
                    for p in range(pairs)]

    def write_state(c):
        r = seq["r"]
        for p in range(pairs):
            n = c * pairs + p
            u = u_r[n]
            v_new = jnp.concatenate([u[:CHUNK] - r[p][0][:CHUNK], u[CHUNK:] - r[p][1][:CHUNK]], axis=0).astype(BF16)
            o_intra = dot(aqk_r[n], v_new)
            v_blk = jnp.concatenate([v_new, v_new], axis=1) * own_ref[...]
            s_ref[p] = s_ref[p] * keep_r[n] + dot(kdt_r[n], v_blk)
            for hh in range(2):
                lo = (2 * p + hh) * HEAD_DIM
                rows = slice(c * CHUNK, (c + 1) * CHUNK)
                o = r[p][hh][CHUNK:] + o_intra[hh * CHUNK:(hh + 1) * CHUNK]
                o = _rms(o) * onorm * _silu(z_ref[0, rows, lo:lo + HEAD_DIM].astype(F32))
                o_ref[0, rows, lo:lo + HEAD_DIM] = o.astype(o_ref.dtype)

    prep = [stage_a]
    for level in range(ml_ref.shape[0]):
        prep += [functools.partial(level_ft, level), functools.partial(level_t, level)]
    prep.append(stage_sol)
    recur = [f for c in range(cb) for f in (functools.partial(read_state, c), functools.partial(write_state, c))]
    prep = prep if do_prep else []
    recur = recur if do_recur else []
    for i in range(max(len(prep), len(recur))):
        if i < len(recur):
            recur[i]()
        if i < len(prep):
            prep[i]()


def _gdn_kernel(*refs, n_time_blocks):
    ins, o_ref, (s_ref, xf_s), bufs = refs[:16], refs[16], refs[17:19], refs[19:]
    sets = bufs[:5], bufs[5:]
    t = pl.program_id(2)
    last = n_time_blocks
    step = functools.partial(_gdn_step, *ins, o_ref, s_ref, xf_s)

    @pl.when(t == 0)
    def _():
        s_ref[...] = jnp.zeros_like(s_ref)
        step(sets[0], sets[1], True, do_recur=False)

    for parity in range(2):
        @pl.when((t > 0) & (t < last) & (t % 2 == parity))
        def _():
            step(sets[parity], sets[1 - parity], False)

    @pl.when(t == last)
    def _():
        step(sets[last % 2], sets[1 - last % 2], False, do_prep=False)


def gdn_mix(qkv, z, conv_w, beta, gcum, o_norm):
    b, lp, _ = qkv.shape
    nc = lp // CHUNK
    cb, pairs = GDN_BLOCK_CHUNKS, GDN_PAIRS
    assert nc % cb == 0 and GDN_QK_HEADS % pairs == 0
    nt, rb, ng = nc // cb, cb * CHUNK, GDN_QK_HEADS // pairs
    rows = lambda t: (t.reshape(b, nt, cb, CHUNK, GDN_QK_HEADS, 2).transpose(0, 4, 1, 2, 5, 3)
                      .reshape(b, GDN_QK_HEADS, nt, cb, PAIR))
    qw, vw = pairs * HEAD_DIM, 2 * pairs * HEAD_DIM
    k_off, v_off = GDN_QK_DIM // qw, 2 * GDN_QK_DIM // vw
    cur = lambda t: jnp.minimum(t, nt - 1)
    prv = lambda t: jnp.maximum(t - 1, 0)
    seq = lambda w, off: pl.BlockSpec((1, rb, w), lambda i, j, t, off=off: (i, cur(t), off + j))
    halo = lambda w, off: pl.BlockSpec(
        (1, 16, w), lambda i, j, t, off=off: (i, jnp.maximum(cur(t) * (rb // 16) - 1, 0), off + j))
    cw = lambda w, off: pl.BlockSpec((GDN_CONV_K, w), lambda i, j, t, off=off: (0, off + j))
    gate = pl.BlockSpec((1, pairs, 1, cb, PAIR), lambda i, j, t: (i, j, cur(t), 0, 0))
    n_units = cb * pairs
    masks = _gdn_masks()
    handover = [pltpu.VMEM((n_units, PAIR, HEAD_DIM), F32),
                pltpu.VMEM((n_units, 2, PAIR, HEAD_DIM), BF16),
                pltpu.VMEM((n_units, PAIR, PAIR), BF16),
                pltpu.VMEM((n_units, HEAD_DIM, PAIR), BF16),
                pltpu.VMEM((n_units, 1, 2 * HEAD_DIM), F32)]
    return pl.pallas_call(
        functools.partial(_gdn_kernel, n_time_blocks=nt),
        grid=(b, ng, nt + 1),
        in_specs=[seq(qw, 0), seq(qw, k_off), seq(vw, v_off),
                  pl.BlockSpec((1, rb, vw), lambda i, j, t: (i, prv(t), j)),
                  halo(qw, 0), halo(qw, k_off), halo(vw, v_off),
                  cw(qw, 0), cw(qw, k_off), cw(vw, v_off),
                  gate, gate, pl.BlockSpec((1, HEAD_DIM), lambda i, j, t: (0, 0))]
                 + [pl.BlockSpec(mk.shape, lambda i, j, t, nd=mk.ndim: (0,) * nd) for mk in masks],
        out_specs=pl.BlockSpec((1, rb, vw), lambda i, j, t: (i, prv(t), j)),
        out_shape=jax.ShapeDtypeStruct((b, lp, GDN_V_DIM), BF16),
        scratch_shapes=[pltpu.VMEM((pairs, HEAD_DIM, 2 * HEAD_DIM), F32),
                        pltpu.VMEM((rb + 8, 2 * qw + vw), F32)] + handover + handover,
        compiler_params=_params("parallel", "parallel", "arbitrary"),
        name="gdn_mix",
    )(qkv, qkv, qkv, z, qkv, qkv, qkv, conv_w, conv_w, conv_w, rows(beta), rows(gcum),
      o_norm.reshape(1, HEAD_DIM), *masks)


def _rope(x, cos, sin_lo, sin_hi):
    half = ROT_DIM // 2
    return x * cos + pltpu.roll(x, half, 1) * sin_hi + pltpu.roll(x, HEAD_DIM - half, 1) * sin_lo


def _attn_kernel(q_ref, k_ref, v_ref, cos_ref, slo_ref, shi_ref, lam_ref, subln_ref, o_ref,
                 kt_s, kh_s, m_s, l_s, acc_s, sc_s, *, n_blocks, lambda_init):
    tb = ATT_BLOCK
    lam = lam_ref[...]
    lam_val = (jnp.exp(jnp.sum(lam[0:1] * lam[1:2], axis=-1, keepdims=True))
               - jnp.exp(jnp.sum(lam[2:3] * lam[3:4], axis=-1, keepdims=True)) + lambda_init)
    subln = subln_ref[...]
    maps = range(2)

    def roped(ref, r0, n, scale):
        x = ref[0, pl.ds(r0, n), :].astype(F32)
        tabs = (cos_ref[pl.ds(r0, n), :], slo_ref[pl.ds(r0, n), :], shi_ref[pl.ds(r0, n), :])
        return [_rope(x[:, m * HEAD_DIM:(m + 1) * HEAD_DIM], *tabs) * scale for m in maps]

    def dot(a, b):
        return jnp.dot(a, b, preferred_element_type=F32)

    def dot_t(a, b):
        return lax.dot_general(a, b, (((1,), (1,)), ((), ())), preferred_element_type=F32)

    kh = roped(k_ref, 0, CHUNK, 1.0)
    for m in maps:
        kh_s[:, m * HEAD_DIM:(m + 1) * HEAD_DIM] = kh[m].astype(BF16)
    for i in range(n_blocks):
        kb = roped(k_ref, CHUNK + i * tb, tb, 1.0)
        for m in maps:
            kt_s[i, m * HEAD_DIM:(m + 1) * HEAD_DIM, :] = kb[m].T.astype(BF16)

    def start(qm, nq, qrow0, causal_head):
        kpos = lax.broadcasted_iota(jnp.int32, (nq, CHUNK), 1)
        ok = kpos >= LEAD
        if causal_head:
            ok = ok & (kpos <= qrow0 + lax.broadcasted_iota(jnp.int32, (nq, CHUNK), 0))
        vh = v_ref[0, pl.ds(0, CHUNK), :]
        s = [jnp.where(ok, dot_t(qm[m], kh_s[:, m * HEAD_DIM:(m + 1) * HEAD_DIM]), MASK_VALUE) for m in maps]
        mx = [jnp.max(s[m], axis=-1, keepdims=True) for m in maps]
        p = [jnp.exp2(s[m] - mx[m]) for m in maps]
        lane0 = lax.broadcasted_iota(jnp.int32, (nq, LANES), 1) == 0
        for m in maps:
            m_s[m, pl.ds(0, nq), :] = jnp.broadcast_to(mx[m], (nq, LANES))
            l_s[m, pl.ds(0, nq), :] = jnp.where(lane0, jnp.sum(p[m], axis=-1, keepdims=True), 0.0)
            acc_s[m, pl.ds(0, nq), :] = dot(p[m].astype(BF16), vh)

    def lane_cols(x):
        return [x[:, c * LANES:(c + 1) * LANES] for c in range(x.shape[1] // LANES)]

    def scores(qm, j):
        return [dot(qm[m], kt_s[j, m * HEAD_DIM:(m + 1) * HEAD_DIM, :]) for m in maps]

    def update(s, j, diag):
        vb = v_ref[0, pl.ds(pl.multiple_of(CHUNK + j * tb, CHUNK), tb), :]
        if diag:
            ok = (lax.broadcasted_iota(jnp.int32, (tb, tb), 1) <= lax.broadcasted_iota(jnp.int32, (tb, tb), 0))
            s = [jnp.where(ok, s[m], MASK_VALUE) for m in maps]
        sc = [lane_cols(s[m]) for m in maps]
        m_old = [m_s[m] for m in maps]
        m_new = [jnp.maximum(m_old[m], jnp.max(functools.reduce(jnp.maximum, sc[m]), axis=-1, keepdims=True))
                 for m in maps]
        pc = [[jnp.exp2(x - m_new[m]) for x in sc[m]] for m in maps]
        alpha = [jnp.exp2(m_old[m] - m_new[m]) for m in maps]
        pv = [dot(jnp.concatenate(pc[m], axis=1).astype(BF16), vb) for m in maps]
        for m in maps:
            m_s[m] = m_new[m]
            l_s[m] = alpha[m] * l_s[m] + functools.reduce(jnp.add, pc[m])
            acc_s[m] = jnp.concatenate([alpha[m]] * (DIFF_V_DIM // LANES), axis=1) * acc_s[m] + pv[m]

    def finish(r0, nq):
        l = [jnp.sum(l_s[m, pl.ds(0, nq), :], axis=-1, keepdims=True) for m in maps]
        o = acc_s[0, pl.ds(0, nq), :] / l[0] - lam_val * (acc_s[1, pl.ds(0, nq), :] / l[1])
        o_ref[0, pl.ds(r0, nq), :] = (_rms(o) * subln * (1.0 - lambda_init)).astype(o_ref.dtype)

    scale = HEAD_DIM ** -0.5 * math.log2(math.e)
    bf = lambda xs: [x.astype(BF16) for x in xs]
    start(bf(roped(q_ref, 0, CHUNK, scale)), CHUNK, 0, True)
    finish(0, CHUNK)

    def q_block(qi, carry):
        r0 = pl.multiple_of(CHUNK + qi * tb, CHUNK)
        qm = bf(roped(q_ref, r0, tb, scale))
        start(qm, tb, r0, False)

        def produce(slot, j):
            for m, s in enumerate(scores(qm, j)):
                sc_s[slot, m] = s

        def consume(slot, j, diag):
            update([sc_s[slot, m] for m in maps], j, diag)

        produce(0, 0)

        def kv_pair(jj, c2):
            j = 2 * jj
            produce(1, j + 1)
            consume(0, j, False)
            produce(0, j + 2)
            consume(1, j + 1, False)
            return c2

        lax.fori_loop(0, qi // 2, kv_pair, 0)

        @pl.when(qi % 2 == 0)
        def _():
            consume(0, qi, True)

        @pl.when(qi % 2 == 1)
        def _():
            produce(1, qi)
            consume(0, qi - 1, False)
            consume(1, qi, True)

        finish(r0, tb)
        return carry

    lax.fori_loop(0, n_blocks, q_block, 0)


def diff_attention(q, k, v, rope_tabs, lam, subln, lambda_init):
    b, lp, _ = q.shape
    n_blocks = (lp - CHUNK) // ATT_BLOCK
    assert CHUNK + n_blocks * ATT_BLOCK == lp
    head = pl.BlockSpec((1, lp, 2 * HEAD_DIM), lambda i, j: (i, 0, j))
    tab = pl.BlockSpec((lp, HEAD_DIM), lambda i, j: (0, 0))
    return pl.pallas_call(
        functools.partial(_attn_kernel, n_blocks=n_blocks, lambda_init=lambda_init),
        grid=(b, DIFF_HEADS),
        in_specs=[head, head, head, tab, tab, tab,
                  pl.BlockSpec((4, HEAD_DIM), lambda i, j: (0, 0)),
                  pl.BlockSpec((1, DIFF_V_DIM), lambda i, j: (0, 0))],
        out_specs=head,
        out_shape=jax.ShapeDtypeStruct((b, lp, DIFF_HEADS * DIFF_V_DIM), BF16),
        scratch_shapes=[pltpu.VMEM((n_blocks, 2 * HEAD_DIM, ATT_BLOCK), BF16),
                        pltpu.VMEM((CHUNK, 2 * HEAD_DIM), BF16),
                        pltpu.VMEM((2, ATT_BLOCK, LANES), F32),
                        pltpu.VMEM((2, ATT_BLOCK, LANES), F32),
                        pltpu.VMEM((2, ATT_BLOCK, DIFF_V_DIM), F32),
                        pltpu.VMEM((2, 2, ATT_BLOCK, ATT_BLOCK), F32)],
        compiler_params=_params("parallel", "parallel"),
        name="diff_attention",
    )(q, k, v, *rope_tabs, lam, subln.reshape(1, DIFF_V_DIM))


def _rope_tables(lp):
    half = ROT_DIM // 2
    pos = jnp.maximum(jnp.arange(lp, dtype=F32) - LEAD, 0.0)
    inv_freq = ROPE_THETA ** (-jnp.arange(0, ROT_DIM, 2, dtype=F32) / ROT_DIM)
    ang = pos[:, None] * inv_freq[None, :]
    c, s = jnp.cos(ang), jnp.sin(ang)
    zeros = jnp.zeros((lp, HEAD_DIM - ROT_DIM), F32)
    cos = jnp.concatenate([c, c, jnp.ones_like(zeros)], axis=1)
    sin_lo = jnp.concatenate([-s, jnp.zeros((lp, half), F32), zeros], axis=1)
    sin_hi = jnp.concatenate([jnp.zeros((lp, half), F32), s, zeros], axis=1)
    return cos, sin_lo, sin_hi


def kernel(x, meta_tokens, norm_gains, mlp_w_up, mlp_w_down, gdn_w_in, gdn_conv_w, gdn_a_log, gdn_dt_bias,
           gdn_o_norm, gdn_w_out, kv_norm, w_kv, diff_w_q, diff_lambda, diff_subln, diff_w_o):
    b, seq, d = x.shape
    lp = LEAD + N_META + seq
    m = b * lp
    meta = jnp.broadcast_to(meta_tokens.astype(x.dtype)[None], (b, N_META, d))
    h = jnp.concatenate([jnp.zeros((b, LEAD, d), x.dtype), meta, x], axis=1).reshape(m, d)
    rope_tabs = _rope_tables(lp)
    wb = lambda w: w.astype(BF16)
    w_up_b, w_down_b, w_in_all, w_out_b = wb(mlp_w_up), wb(mlp_w_down), wb(gdn_w_in), wb(gdn_w_out)
    w_q_b, w_o_b = wb(diff_w_q), wb(diff_w_o)

    hn = rmsnorm_rows(h, norm_gains[0, 0])
    kv_k = kv_v = None
    for layer in range(DEPTH):
        if layer < N_A_LAYERS:
            w_in = gdn_w_in[layer]
            qkv = matmul(hn, w_in_all, BF16, cols=(0, GDN_CONV_DIM), layer=layer)
            z = matmul(hn, w_in_all, BF16, cols=(GDN_CONV_DIM, GDN_CONV_DIM + GDN_V_DIM), layer=layer)
            w_ba = jnp.zeros((d, 2 * LANES), F32)
            w_ba = w_ba.at[:, :GDN_V_HEADS].set(w_in[:, GDN_CONV_DIM + GDN_V_DIM:GDN_CONV_DIM + GDN_V_DIM + GDN_V_HEADS])
            w_ba = w_ba.at[:, LANES:LANES + GDN_V_HEADS].set(w_in[:, GDN_CONV_DIM + GDN_V_DIM + GDN_V_HEADS:])
            ba = matmul(hn, wb(w_ba), F32)
            beta, gcum = gdn_gates(ba.reshape(b, lp, 2 * LANES), gdn_a_log[layer], gdn_dt_bias[layer])
            o = gdn_mix(qkv.reshape(b, lp, GDN_CONV_DIM), z.reshape(b, lp, GDN_V_DIM), gdn_conv_w[layer],
                        beta[..., :GDN_V_HEADS], gcum[..., :GDN_V_HEADS], gdn_o_norm[layer])
            mix = matmul(o.reshape(m, GDN_V_DIM), w_out_b, BF16, layer=layer)
        else:
            j = layer - N_A_LAYERS
            lambda_init = 0.8 - 0.6 * math.exp(-0.3 * layer)
            q = matmul(hn, w_q_b, BF16, layer=j)
            o = diff_attention(q.reshape(b, lp, DIFF_Q_DIM), kv_k, kv_v, rope_tabs, diff_lambda[j],
                               diff_subln[j], lambda_init)
            mix = matmul(o.reshape(m, DIFF_HEADS * DIFF_V_DIM), w_o_b, BF16, layer=j)
        h, hn = resid_norm(h, mix, norm_gains[layer, 1], norm_gains[layer, 2:3])
        up = matmul(hn, w_up_b, BF16, relu2=True, layer=layer)
        ff = matmul(up, w_down_b, BF16, layer=layer)
        if layer == N_A_LAYERS - 1:
            g_next = jnp.stack([norm_gains[layer + 1, 0], kv_norm])
            h, hn, hkv = resid_norm(h, ff, norm_gains[layer, 3], g_next)
            w_kv_b = wb(w_kv)
            kv_k = matmul(hkv, w_kv_b, BF16, cols=(0, DIFF_Q_DIM)).reshape(b, lp, DIFF_Q_DIM)
            kv_v = matmul(hkv, w_kv_b, BF16, cols=(DIFF_Q_DIM, w_kv.shape[1])).reshape(b, lp, DIFF_HEADS * DIFF_V_DIM)
        elif layer + 1 < DEPTH:
            h, hn = resid_norm(h, ff, norm_gains[layer, 3], norm_gains[layer + 1, 0:1])
        else:
            h, = resid_norm(h, ff, norm_gains[layer, 3], None)
    return h.reshape(b, lp, d)[:, LEAD + N_META:]
```

```python
import functools
import math

import jax
import jax.numpy as jnp
import numpy as np
from jax import lax
from jax.experimental import pallas as pl
from jax.experimental.pallas import tpu as pltpu

F32 = jnp.float32
BF16 = jnp.bfloat16

D_MODEL = 2048
DEPTH = 4
N_A_LAYERS = DEPTH // 2
N_META = 16
D_FF = 4 * D_MODEL
HEAD_DIM = 128
GDN_QK_HEADS = D_MODEL // HEAD_DIM
GDN_V_HEADS = 2 * GDN_QK_HEADS
GDN_QK_DIM = GDN_QK_HEADS * HEAD_DIM
GDN_V_DIM = GDN_V_HEADS * HEAD_DIM
GDN_CONV_DIM = 2 * GDN_QK_DIM + GDN_V_DIM
GDN_CONV_K = 4
CHUNK = 64
LEAD = CHUNK - N_META
PAIR = 2 * CHUNK
DIFF_HEADS = D_MODEL // 256
DIFF_V_DIM = 2 * HEAD_DIM
DIFF_Q_DIM = DIFF_HEADS * 2 * HEAD_DIM
ROT_DIM = HEAD_DIM // 4
ROPE_THETA = 500000.0
EPS = 1e-6
LANES = 128
GDN_PAIRS = 2
GDN_BLOCK_CHUNKS = 5
ATT_BLOCK = 512
MASK_VALUE = -1e30
VMEM_LIMIT_BYTES = 56 * 1024 * 1024


def _params(*sem):
    return pltpu.CompilerParams(dimension_semantics=sem, vmem_limit_bytes=VMEM_LIMIT_BYTES)


def _pick(n, cands):
    for c in cands:
        if n % c == 0:
            return c
    return n


def _rms(x):
    return x * lax.rsqrt(jnp.mean(x * x, axis=-1, keepdims=True) + EPS)


def _rmsnorm_kernel(x_ref, g_ref, o_ref):
    o_ref[...] = (_rms(x_ref[...]) * g_ref[...]).astype(o_ref.dtype)


def rmsnorm_rows(x, gain):
    m, d = x.shape
    tm = _pick(m, (640, 512, 320, 256, 128, 64, 32, 16, 8))
    return pl.pallas_call(
        _rmsnorm_kernel,
        grid=(m // tm,),
        in_specs=[pl.BlockSpec((tm, d), lambda i: (i, 0)), pl.BlockSpec((1, d), lambda i: (0, 0))],
        out_specs=pl.BlockSpec((tm, d), lambda i: (i, 0)),
        out_shape=jax.ShapeDtypeStruct((m, d), BF16),
        compiler_params=_params("parallel"),
        name="rmsnorm_rows",
    )(x, gain.reshape(1, d))


def _resid_norm_kernel(h_ref, y_ref, gp_ref, gn_ref, *out_refs, n_next):
    h_new = h_ref[...] + _rms(y_ref[...].astype(F32)) * gp_ref[...]
    out_refs[0][...] = h_new
    if n_next:
        hn = _rms(h_new)
        for j in range(n_next):
            out_refs[1 + j][...] = (hn * gn_ref[j:j + 1, :]).astype(BF16)


def resid_norm(h, y, g_post, g_next):
    m, d = h.shape
    n_next = 0 if g_next is None else g_next.shape[0]
    gn = jnp.zeros((1, d), F32) if g_next is None else g_next
    tm = _pick(m, (640, 512, 320, 256, 128, 64, 32, 16, 8))
    row = pl.BlockSpec((tm, d), lambda i: (i, 0))
    outs = pl.pallas_call(
        functools.partial(_resid_norm_kernel, n_next=n_next),
        grid=(m // tm,),
        in_specs=[row, row, pl.BlockSpec((1, d), lambda i: (0, 0)),
                  pl.BlockSpec(gn.shape, lambda i: (0, 0))],
        out_specs=[row] * (1 + n_next),
        out_shape=[jax.ShapeDtypeStruct((m, d), F32)] + [jax.ShapeDtypeStruct((m, d), BF16)] * n_next,
        compiler_params=_params("parallel"),
        name="resid_norm",
    )(h, y, g_post.reshape(1, d), gn)
    return outs


def _mm_kernel(x_ref, w_ref, o_ref, *acc, nk, relu2):
    def finish(r):
        if relu2:
            r = jnp.square(jnp.maximum(r, 0.0))
        o_ref[...] = r.astype(o_ref.dtype)

    part = jnp.dot(x_ref[...], w_ref[...], preferred_element_type=F32)
    if nk == 1:
        finish(part)
        return
    acc_ref, = acc
    k = pl.program_id(2)

    @pl.when(k == 0)
    def _():
        acc_ref[...] = part

    @pl.when(k > 0)
    def _():
        acc_ref[...] += part

    @pl.when(k == nk - 1)
    def _():
        finish(acc_ref[...])


def matmul(x, w, out_dtype, relu2=False, cols=None, layer=None):
    m, kdim = x.shape
    lo, hi = cols or (0, w.shape[-1])
    n = hi - lo
    tm = _pick(m, (1280, 1024, 640, 512, 320, 256, 128, 64, 32, 16, 8))
    tn = _pick(math.gcd(n, lo) if lo else n, (1024, 512, 256, 128))
    tk = kdim if kdim <= 4096 else 2048
    nk = kdim // tk
    assert lo % tn == 0 and n % tn == 0 and (w.ndim == 3) == (layer is not None)
    first = lo // tn
    if layer is None:
        w_spec = pl.BlockSpec((tk, tn), lambda i, j, k: (k, first + j))
    else:
        w_spec = pl.BlockSpec((None, tk, tn), lambda i, j, k: (layer, k, first + j))
    return pl.pallas_call(
        functools.partial(_mm_kernel, nk=nk, relu2=relu2),
        grid=(m // tm, n // tn, nk),
        in_specs=[pl.BlockSpec((tm, tk), lambda i, j, k: (i, k)), w_spec],
        out_specs=pl.BlockSpec((tm, tn), lambda i, j, k: (i, j)),
        out_shape=jax.ShapeDtypeStruct((m, n), out_dtype),
        scratch_shapes=[pltpu.VMEM((tm, tn), F32)] if nk > 1 else [],
        compiler_params=_params("parallel", "parallel", "arbitrary"),
        name="matmul",
    )(x, w)


def _mm_f32w_kernel(x_ref, w_ref, o_ref, wb_ref, *, relu2):
    @pl.when(pl.program_id(1) == 0)
    def _():
        wb_ref[...] = w_ref[...].astype(BF16)

    r = jnp.dot(x_ref[...], wb_ref[...], preferred_element_type=F32)
    if relu2:
        r = jnp.square(jnp.maximum(r, 0.0))
    o_ref[...] = r.astype(o_ref.dtype)


def matmul_f32w(x, w, out_dtype, relu2=False, cols=None, layer=None):
    m, kdim = x.shape
    lo, hi = cols or (0, w.shape[-1])
    n = hi - lo
    tm = _pick(m, (1280, 1024, 640, 512, 320, 256, 128, 64, 32, 16, 8))
    tn = _pick(math.gcd(n, lo) if lo else n, (1024, 512, 256, 128))
    assert lo % tn == 0 and n % tn == 0 and (w.ndim == 3) == (layer is not None)
    first = lo // tn
    if layer is None:
        w_spec = pl.BlockSpec((kdim, tn), lambda j, i: (0, first + j))
    else:
        w_spec = pl.BlockSpec((None, kdim, tn), lambda j, i: (layer, 0, first + j))
    return pl.pallas_call(
        functools.partial(_mm_f32w_kernel, relu2=relu2),
        grid=(n // tn, m // tm),
        in_specs=[pl.BlockSpec((tm, kdim), lambda j, i: (i, 0)), w_spec],
        out_specs=pl.BlockSpec((tm, tn), lambda j, i: (i, j)),
        out_shape=jax.ShapeDtypeStruct((m, n), out_dtype),
        scratch_shapes=[pltpu.VMEM((kdim, tn), BF16)],
        compiler_params=_params("parallel", "arbitrary"),
        name="matmul_f32w",
    )(x, w)


def _gates_kernel(ba_ref, alog_ref, dtb_ref, beta_ref, gcum_ref, *, n_chunks):
    row = lax.broadcasted_iota(jnp.int32, (CHUNK, CHUNK), 0)
    col = lax.broadcasted_iota(jnp.int32, (CHUNK, CHUNK), 1)
    tril = (row >= col).astype(F32)
    neg_rate = -jnp.exp(alog_ref[...])
    dtb = dtb_ref[...]

    def body(c, carry):
        r0 = pl.multiple_of(c * CHUNK, CHUNK)
        ba = ba_ref[0, pl.ds(r0, CHUNK), :]
        b, a = ba[:, :LANES], ba[:, LANES:]
        pos = r0 + lax.broadcasted_iota(jnp.int32, (CHUNK, LANES), 0)
        live = pos >= LEAD
        x = a + dtb
        softplus = jnp.maximum(x, 0.0) + jnp.log(1.0 + jnp.exp(-jnp.abs(x)))
        g = jnp.where(live, neg_rate * softplus, 0.0)
        beta_ref[0, pl.ds(r0, CHUNK), :] = jnp.where(live, 1.0 / (1.0 + jnp.exp(-b)), 0.0)
        gcum_ref[0, pl.ds(r0, CHUNK), :] = jnp.dot(tril, g, precision=lax.Precision.HIGHEST,
                                                   preferred_element_type=F32)
        return carry

    lax.fori_loop(0, n_chunks, body, 0)


def gdn_gates(ba, a_log, dt_bias):
    b, lp, _ = ba.shape
    pad = lambda t: jnp.zeros((1, LANES), F32).at[0, :GDN_V_HEADS].set(t.astype(F32))
    blk = pl.BlockSpec((1, lp, LANES), lambda i: (i, 0, 0))
    return pl.pallas_call(
        functools.partial(_gates_kernel, n_chunks=lp // CHUNK),
        grid=(b,),
        in_specs=[pl.BlockSpec((1, lp, 2 * LANES), lambda i: (i, 0, 0)),
                  pl.BlockSpec((1, LANES), lambda i: (0, 0)), pl.BlockSpec((1, LANES), lambda i: (0, 0))],
        out_specs=[blk, blk],
        out_shape=[jax.ShapeDtypeStruct((b, lp, LANES), F32)] * 2,
        compiler_params=_params("parallel"),
        name="gdn_gates",
    )(ba, pad(a_log), pad(dt_bias))


def _silu(x):
    return x / (1.0 + jnp.exp(-x))


def _gdn_masks():
    ri, ci = np.indices((PAIR, PAIR))
    same_head = (ri // CHUNK) == (ci // CHUNK)

    def sub_blocks(blk):
        return ((ri // blk) % 2 == 1) & ((ci // blk) % 2 == 0) & ((ri // (2 * blk)) == (ci // (2 * blk)))

    levels = []
    blk = 2
    while blk < CHUNK:
        levels.append(sub_blocks(blk))
        blk *= 2
    f32_masks = np.stack([ri == ci, ri != ci, sub_blocks(1)]).astype(np.float32)
    causal = np.where(same_head & (ri >= ci), 0.0, -np.inf).astype(np.float32)
    rr, cc = np.indices((PAIR, 2 * HEAD_DIM))
    own_cols = (rr // CHUNK) == (cc // HEAD_DIM)
    return (jnp.asarray(np.concatenate([f32_masks, causal[None]])), jnp.asarray(np.stack(levels), BF16),
            jnp.asarray(own_cols, BF16))


def _gdn_step(q_ref, k_ref, v_ref, z_ref, qp_ref, kp_ref, vp_ref, wq_ref, wk_ref, wv_ref, brow_ref, grow_ref,
              onorm_ref, mf_ref, ml_ref, own_ref, o_ref, s_ref, xf_s, wr, rd, first_block,
              do_prep=True, do_recur=True):
    cb, pairs = GDN_BLOCK_CHUNKS, GDN_PAIRS
    u_w, wq_w, aqk_w, kdt_w, keep_w = wr
    u_r, wq_r, aqk_r, kdt_r, keep_r = rd
    eye, off_diag, sub1, causal = mf_ref[0], mf_ref[1], mf_ref[2], mf_ref[3]
    head1_lane = lax.broadcasted_iota(jnp.int32, (1, 2 * HEAD_DIM), 1) >= HEAD_DIM
    head1_row = lax.broadcasted_iota(jnp.int32, (PAIR, 1), 0) >= CHUNK

    def dot(a, b):
        return jnp.dot(a, b, preferred_element_type=F32)

    def dot_t(a, b):
        return lax.dot_general(a, b, (((1,), (1,)), ((), ())), preferred_element_type=F32)

    def to_col(row):
        return jnp.sum(row * eye, axis=-1, keepdims=True)

    qw, vw = pairs * HEAD_DIM, 2 * pairs * HEAD_DIM
    seen_rows = jnp.where(first_block, 0.0, 1.0)
    for ref, halo_ref, lo, width in ((q_ref, qp_ref, 0, qw), (k_ref, kp_ref, qw, qw), (v_ref, vp_ref, 2 * qw, vw)):
        if do_prep:
            xf_s[0:8, lo:lo + width] = halo_ref[0].astype(F32)[8:, :] * seen_rows
            xf_s[8:, lo:lo + width] = ref[0].astype(F32)

    def conv_part(w_ref, c, lo, wlo, width):
        w = w_ref[:, wlo:wlo + width]
        y = None
        for s in range(GDN_CONV_K):
            r0 = 8 + c * CHUNK - s
            term = xf_s[r0:r0 + CHUNK, lo:lo + width] * w[GDN_CONV_K - 1 - s:GDN_CONV_K - s, :]
            y = term if y is None else y + term
        return _silu(y)

    units = [(c, p) for c in range(cb) for p in range(pairs)]
    st = dict(a_b=[], k2=[], q2=[], kb=[], v2=[], beta=[], g_col=[], g_last=[], ts=[], tbs=[], fts=[])

    def stage_a():
        for n, (c, p) in enumerate(units):
            q = conv_part(wq_ref, c, p * HEAD_DIM, p * HEAD_DIM, HEAD_DIM)
            k = conv_part(wk_ref, c, qw + p * HEAD_DIM, p * HEAD_DIM, HEAD_DIM)
            v = conv_part(wv_ref, c, 2 * qw + 2 * p * HEAD_DIM, 2 * p * HEAD_DIM, 2 * HEAD_DIM)
            q = q * lax.rsqrt(jnp.sum(q * q, axis=-1, keepdims=True) + EPS) * HEAD_DIM ** -0.5
            k = k * lax.rsqrt(jnp.sum(k * k, axis=-1, keepdims=True) + EPS)
            g_row = grow_ref[0, p, 0, c:c + 1, :]
            g_col = to_col(g_row)
            beta = to_col(brow_ref[0, p, 0, c:c + 1, :])
            gl = [g_row[:, (hh + 1) * CHUNK - 1:(hh + 1) * CHUNK] for hh in range(2)]
            g_last = jnp.where(head1_row, gl[1], gl[0])
            keep_w[n] = jnp.where(head1_lane, jnp.exp(gl[1]), jnp.exp(gl[0]))
            decay = jnp.exp((g_col - g_row) + causal)
            k2 = jnp.concatenate([k, k], axis=0)
            q2 = jnp.concatenate([q, q], axis=0)
            kb = k2 * beta
            k2b = k2.astype(BF16)
            a_kk = dot_t(kb.astype(BF16), k2b) * decay * off_diag
            aqk_w[n] = (dot_t(q2.astype(BF16), k2b) * decay).astype(BF16)
            st["a_b"].append(a_kk.astype(BF16))
            st["ts"].append(eye - a_kk * sub1)
            st["k2"].append(k2)
            st["q2"].append(q2)
            st["kb"].append(kb)
            st["beta"].append(beta)
            st["v2"].append(jnp.concatenate([v[:, :HEAD_DIM], v[:, HEAD_DIM:]], axis=0))
            st["g_col"].append(g_col)
            st["g_last"].append(g_last)

    def level_ft(level):
        mask = ml_ref[level]
        st["tbs"] = [t.astype(BF16) for t in st["ts"]]
        st["fts"] = [dot(a * mask, tb).astype(BF16) for a, tb in zip(st["a_b"], st["tbs"])]

    def level_t(blk):
        st["ts"] = [t - dot(tb, ft) for t, tb, ft in zip(st["ts"], st["tbs"], st["fts"])]

    def stage_sol():
        for n in range(len(units)):
            e_g = jnp.exp(st["g_col"][n])
            rhs = jnp.concatenate([st["v2"][n] * st["beta"][n], st["kb"][n] * e_g], axis=1).astype(BF16)
            sol = dot(st["ts"][n].astype(BF16), rhs)
            u_w[n] = sol[:, :HEAD_DIM]
            w = sol[:, HEAD_DIM:].astype(BF16)
            qe = (st["q2"][n] * e_g).astype(BF16)
            wq_w[n, 0] = jnp.concatenate([w[:CHUNK], qe[:CHUNK]], axis=0)
            wq_w[n, 1] = jnp.concatenate([w[CHUNK:], qe[CHUNK:]], axis=0)
            kdt_w[n] = (st["k2"][n] * jnp.exp(st["g_last"][n] - st["g_col"][n])).T.astype(BF16)

    onorm = onorm_ref[...]
    seq = {}

    def read_state(c):
        sb = [s_ref[p].astype(BF16) for p in range(pairs)]
        seq["r"] = [[dot(wq_r[c * pairs + p, hh], sb[p][:, hh * HEAD_DIM:(hh + 1) * HEAD_DIM]) for hh in range(2)]
                    for p in range(pairs)]

    def write_state(c):
        r = seq["r"]
        for p in range(pairs):
            n = c * pairs + p
            u = u_r[n]
            v_new = jnp.concatenate([u[:CHUNK] - r[p][0][:CHUNK], u[CHUNK:] - r[p][1][:CHUNK]], axis=0).astype(BF16)
            o_intra = dot(aqk_r[n], v_new)
            v_blk = jnp.concatenate([v_new, v_new], axis=1) * own_ref[...]
            s_ref[p] = s_ref[p] * keep_r[n] + dot(kdt_r[n], v_blk)
            for hh in range(2):
                lo = (2 * p + hh) * HEAD_DIM
                rows = slice(c * CHUNK, (c + 1) * CHUNK)
                o = r[p][hh][CHUNK:] + o_intra[hh * CHUNK:(hh + 1) * CHUNK]
                o = _rms(o) * onorm * _silu(z_ref[0, rows, lo:lo + HEAD_DIM].astype(F32))
                o_ref[0, rows, lo:lo + HEAD_DIM] = o.astype(o_ref.dtype)

    prep = [stage_a]
    for level in range(ml_ref.shape[0]):
        prep += [functools.partial(level_ft, level), functools.partial(level_t, level)]
    prep.append(stage_sol)
    recur = [f for c in range(cb) for f in (functools.partial(read_state, c), functools.partial(write_state, c))]
    prep = prep if do_prep else []
    recur = recur if do_recur else []
    for i in range(max(len(prep), len(recur))):
        if i < len(recur):
            recur[i]()
        if i < len(prep):
            prep[i]()


def _gdn_kernel(*refs, n_time_blocks):
    ins, o_ref, (s_ref, xf_s), bufs = refs[:16], refs[16], refs[17:19], refs[19:]
    sets = bufs[:5], bufs[5:]
    t = pl.program_id(2)
    last = n_time_blocks
    step = functools.partial(_gdn_step, *ins, o_ref, s_ref, xf_s)

    @pl.when(t == 0)
    def _():
        s_ref[...] = jnp.zeros_like(s_ref)
        step(sets[0], sets[1], True, do_recur=False)

    for parity in range(2):
        @pl.when((t > 0) & (t < last) & (t % 2 == parity))
        def _():
            step(sets[parity], sets[1 - parity], False)

    @pl.when(t == last)
    def _():
        step(sets[last % 2], sets[1 - last % 2], False, do_prep=False)


def gdn_mix(qkv, z, conv_w, beta, gcum, o_norm):
    b, lp, _ = qkv.shape
    nc = lp // CHUNK
    cb, pairs = GDN_BLOCK_CHUNKS, GDN_PAIRS
    assert nc % cb == 0 and GDN_QK_HEADS % pairs == 0
    nt, rb, ng = nc // cb, cb * CHUNK, GDN_QK_HEADS // pairs
    rows = lambda t: (t.reshape(b, nt, cb, CHUNK, GDN_QK_HEADS, 2).transpose(0, 4, 1, 2, 5, 3)
                      .reshape(b, GDN_QK_HEADS, nt, cb, PAIR))
    qw, vw = pairs * HEAD_DIM, 2 * pairs * HEAD_DIM
    k_off, v_off = GDN_QK_DIM // qw, 2 * GDN_QK_DIM // vw
    cur = lambda t: jnp.minimum(t, nt - 1)
    prv = lambda t: jnp.maximum(t - 1, 0)
    seq = lambda w, off: pl.BlockSpec((1, rb, w), lambda i, j, t, off=off: (i, cur(t), off + j))
    halo = lambda w, off: pl.BlockSpec(
        (1, 16, w), lambda i, j, t, off=off: (i, jnp.maximum(cur(t) * (rb // 16) - 1, 0), off + j))
    cw = lambda w, off: pl.BlockSpec((GDN_CONV_K, w), lambda i, j, t, off=off: (0, off + j))
    gate = pl.BlockSpec((1, pairs, 1, cb, PAIR), lambda i, j, t: (i, j, cur(t), 0, 0))
    n_units = cb * pairs
    masks = _gdn_masks()
    handover = [pltpu.VMEM((n_units, PAIR, HEAD_DIM), F32),
                pltpu.VMEM((n_units, 2, PAIR, HEAD_DIM), BF16),
                pltpu.VMEM((n_units, PAIR, PAIR), BF16),
                pltpu.VMEM((n_units, HEAD_DIM, PAIR), BF16),
                pltpu.VMEM((n_units, 1, 2 * HEAD_DIM), F32)]
    return pl.pallas_call(
        functools.partial(_gdn_kernel, n_time_blocks=nt),
        grid=(b, ng, nt + 1),
        in_specs=[seq(qw, 0), seq(qw, k_off), seq(vw, v_off),
                  pl.BlockSpec((1, rb, vw), lambda i, j, t: (i, prv(t), j)),
                  halo(qw, 0), halo(qw, k_off), halo(vw, v_off),
                  cw(qw, 0), cw(qw, k_off), cw(vw, v_off),
                  gate, gate, pl.BlockSpec((1, HEAD_DIM), lambda i, j, t: (0, 0))]
                 + [pl.BlockSpec(mk.shape, lambda i, j, t, nd=mk.ndim: (0,) * nd) for mk in masks],
        out_specs=pl.BlockSpec((1, rb, vw), lambda i, j, t: (i, prv(t), j)),
        out_shape=jax.ShapeDtypeStruct((b, lp, GDN_V_DIM), BF16),
        scratch_shapes=[pltpu.VMEM((pairs, HEAD_DIM, 2 * HEAD_DIM), F32),
                        pltpu.VMEM((rb + 8, 2 * qw + vw), F32)] + handover + handover,
        compiler_params=_params("parallel", "parallel", "arbitrary"),
        name="gdn_mix",
    )(qkv, qkv, qkv, z, qkv, qkv, qkv, conv_w, conv_w, conv_w, rows(beta), rows(gcum),
      o_norm.reshape(1, HEAD_DIM), *masks)


def _rope(x, cos, sin_lo, sin_hi):
    half = ROT_DIM // 2
    return x * cos + pltpu.roll(x, half, 1) * sin_hi + pltpu.roll(x, HEAD_DIM - half, 1) * sin_lo


def _attn_kernel(q_ref, k_ref, v_ref, cos_ref, slo_ref, shi_ref, lam_ref, subln_ref, o_ref,
                 kt_s, kh_s, m_s, l_s, acc_s, sc_s, *, n_blocks, lambda_init):
    tb = ATT_BLOCK
    lam = lam_ref[...]
    lam_val = (jnp.exp(jnp.sum(lam[0:1] * lam[1:2], axis=-1, keepdims=True))
               - jnp.exp(jnp.sum(lam[2:3] * lam[3:4], axis=-1, keepdims=True)) + lambda_init)
    subln = subln_ref[...]
    maps = range(2)

    def roped(ref, r0, n, scale):
        x = ref[0, pl.ds(r0, n), :].astype(F32)
        tabs = (cos_ref[pl.ds(r0, n), :], slo_ref[pl.ds(r0, n), :], shi_ref[pl.ds(r0, n), :])
        return [_rope(x[:, m * HEAD_DIM:(m + 1) * HEAD_DIM], *tabs) * scale for m in maps]

    def dot(a, b):
        return jnp.dot(a, b, preferred_element_type=F32)

    def dot_t(a, b):
        return lax.dot_general(a, b, (((1,), (1,)), ((), ())), preferred_element_type=F32)

    kh = roped(k_ref, 0, CHUNK, 1.0)
    for m in maps:
        kh_s[:, m * HEAD_DIM:(m + 1) * HEAD_DIM] = kh[m].astype(BF16)
    for i in range(n_blocks):
        kb = roped(k_ref, CHUNK + i * tb, tb, 1.0)
        for m in maps:
            kt_s[i, m * HEAD_DIM:(m + 1) * HEAD_DIM, :] = kb[m].T.astype(BF16)

    def start(qm, nq, qrow0, causal_head):
        kpos = lax.broadcasted_iota(jnp.int32, (nq, CHUNK), 1)
        ok = kpos >= LEAD
        if causal_head:
            ok = ok & (kpos <= qrow0 + lax.broadcasted_iota(jnp.int32, (nq, CHUNK), 0))
        vh = v_ref[0, pl.ds(0, CHUNK), :]
        s = [jnp.where(ok, dot_t(qm[m], kh_s[:, m * HEAD_DIM:(m + 1) * HEAD_DIM]), MASK_VALUE) for m in maps]
        mx = [jnp.max(s[m], axis=-1, keepdims=True) for m in maps]
        p = [jnp.exp2(s[m] - mx[m]) for m in maps]
        lane0 = lax.broadcasted_iota(jnp.int32, (nq, LANES), 1) == 0
        for m in maps:
            m_s[m, pl.ds(0, nq), :] = jnp.broadcast_to(mx[m], (nq, LANES))
            l_s[m, pl.ds(0, nq), :] = jnp.where(lane0, jnp.sum(p[m], axis=-1, keepdims=True), 0.0)
            acc_s[m, pl.ds(0, nq), :] = dot(p[m].astype(BF16), vh)

    def lane_cols(x):
        return [x[:, c * LANES:(c + 1) * LANES] for c in range(x.shape[1] // LANES)]

    def scores(qm, j):
        return [dot(qm[m], kt_s[j, m * HEAD_DIM:(m + 1) * HEAD_DIM, :]) for m in maps]

    def update(s, j, diag):
        vb = v_ref[0, pl.ds(pl.multiple_of(CHUNK + j * tb, CHUNK), tb), :]
        if diag:
            ok = (lax.broadcasted_iota(jnp.int32, (tb, tb), 1) <= lax.broadcasted_iota(jnp.int32, (tb, tb), 0))
            s = [jnp.where(ok, s[m], MASK_VALUE) for m in maps]
        sc = [lane_cols(s[m]) for m in maps]
        m_old = [m_s[m] for m in maps]
        m_new = [jnp.maximum(m_old[m], jnp.max(functools.reduce(jnp.maximum, sc[m]), axis=-1, keepdims=True))
                 for m in maps]
        pc = [[jnp.exp2(x - m_new[m]) for x in sc[m]] for m in maps]
        alpha = [jnp.exp2(m_old[m] - m_new[m]) for m in maps]
        pv = [dot(jnp.concatenate(pc[m], axis=1).astype(BF16), vb) for m in maps]
        for m in maps:
            m_s[m] = m_new[m]
            l_s[m] = alpha[m] * l_s[m] + functools.reduce(jnp.add, pc[m])
            acc_s[m] = jnp.concatenate([alpha[m]] * (DIFF_V_DIM // LANES), axis=1) * acc_s[m] + pv[m]

    def finish(r0, nq):
        l = [jnp.sum(l_s[m, pl.ds(0, nq), :], axis=-1, keepdims=True) for m in maps]
        o = acc_s[0, pl.ds(0, nq), :] / l[0] - lam_val * (acc_s[1, pl.ds(0, nq), :] / l[1])
        o_ref[0, pl.ds(r0, nq), :] = (_rms(o) * subln * (1.0 - lambda_init)).astype(o_ref.dtype)

    scale = HEAD_DIM ** -0.5 * math.log2(math.e)
    bf = lambda xs: [x.astype(BF16) for x in xs]
    start(bf(roped(q_ref, 0, CHUNK, scale)), CHUNK, 0, True)
    finish(0, CHUNK)

    def q_block(qi, carry):
        r0 = pl.multiple_of(CHUNK + qi * tb, CHUNK)
        qm = bf(roped(q_ref, r0, tb, scale))
        start(qm, tb, r0, False)

        def produce(slot, j):
            for m, s in enumerate(scores(qm, j)):
                sc_s[slot, m] = s

        def consume(slot, j, diag):
            update([sc_s[slot, m] for m in maps], j, diag)

        produce(0, 0)

        def kv_pair(jj, c2):
            j = 2 * jj
            produce(1, j + 1)
            consume(0, j, False)
            produce(0, j + 2)
            consume(1, j + 1, False)
            return c2

        lax.fori_loop(0, qi // 2, kv_pair, 0)

        @pl.when(qi % 2 == 0)
        def _():
            consume(0, qi, True)

        @pl.when(qi % 2 == 1)
        def _():
            produce(1, qi)
            consume(0, qi - 1, False)
            consume(1, qi, True)

        finish(r0, tb)
        return carry

    lax.fori_loop(0, n_blocks, q_block, 0)


def diff_attention(q, k, v, rope_tabs, lam, subln, lambda_init):
    b, lp, _ = q.shape
    n_blocks = (lp - CHUNK) // ATT_BLOCK
    assert CHUNK + n_blocks * ATT_BLOCK == lp
    head = pl.BlockSpec((1, lp, 2 * HEAD_DIM), lambda i, j: (i, 0, j))
    tab = pl.BlockSpec((lp, HEAD_DIM), lambda i, j: (0, 0))
    return pl.pallas_call(
        functools.partial(_attn_kernel, n_blocks=n_blocks, lambda_init=lambda_init),
        grid=(b, DIFF_HEADS),
        in_specs=[head, head, head, tab, tab, tab,
                  pl.BlockSpec((4, HEAD_DIM), lambda i, j: (0, 0)),
                  pl.BlockSpec((1, DIFF_V_DIM), lambda i, j: (0, 0))],
        out_specs=head,
        out_shape=jax.ShapeDtypeStruct((b, lp, DIFF_HEADS * DIFF_V_DIM), BF16),
        scratch_shapes=[pltpu.VMEM((n_blocks, 2 * HEAD_DIM, ATT_BLOCK), BF16),
                        pltpu.VMEM((CHUNK, 2 * HEAD_DIM), BF16),
                        pltpu.VMEM((2, ATT_BLOCK, LANES), F32),
                        pltpu.VMEM((2, ATT_BLOCK, LANES), F32),
                        pltpu.VMEM((2, ATT_BLOCK, DIFF_V_DIM), F32),
                        pltpu.VMEM((2, 2, ATT_BLOCK, ATT_BLOCK), F32)],
        compiler_params=_params("parallel", "parallel"),
        name="diff_attention",
    )(q, k, v, *rope_tabs, lam, subln.reshape(1, DIFF_V_DIM))


def _rope_tables(lp):
    half = ROT_DIM // 2
    pos = jnp.maximum(jnp.arange(lp, dtype=F32) - LEAD, 0.0)
    inv_freq = ROPE_THETA ** (-jnp.arange(0, ROT_DIM, 2, dtype=F32) / ROT_DIM)
    ang = pos[:, None] * inv_freq[None, :]
    c, s = jnp.cos(ang), jnp.sin(ang)
    zeros = jnp.zeros((lp, HEAD_DIM - ROT_DIM), F32)
    cos = jnp.concatenate([c, c, jnp.ones_like(zeros)], axis=1)
    sin_lo = jnp.concatenate([-s, jnp.zeros((lp, half), F32), zeros], axis=1)
    sin_hi = jnp.concatenate([jnp.zeros((lp, half), F32), s, zeros], axis=1)
    return cos, sin_lo, sin_hi


def kernel(x, meta_tokens, norm_gains, mlp_w_up, mlp_w_down, gdn_w_in, gdn_conv_w, gdn_a_log, gdn_dt_bias,
           gdn_o_norm, gdn_w_out, kv_norm, w_kv, diff_w_q, diff_lambda, diff_subln, diff_w_o):
    b, seq, d = x.shape
    lp = LEAD + N_META + seq
    m = b * lp
    meta = jnp.broadcast_to(meta_tokens.astype(x.dtype)[None], (b, N_META, d))
    h = jnp.concatenate([jnp.zeros((b, LEAD, d), x.dtype), meta, x], axis=1).reshape(m, d)
    rope_tabs = _rope_tables(lp)
    wb = lambda w: w.astype(BF16)
    w_down_b, w_out_b, w_o_b = wb(mlp_w_down), wb(gdn_w_out), wb(diff_w_o)

    hn = rmsnorm_rows(h, norm_gains[0, 0])
    kv_k = kv_v = None
    for layer in range(DEPTH):
        if layer < N_A_LAYERS:
            w_in = gdn_w_in[layer]
            qkv = matmul_f32w(hn, gdn_w_in, BF16, cols=(0, GDN_CONV_DIM), layer=layer)
            z = matmul_f32w(hn, gdn_w_in, BF16, cols=(GDN_CONV_DIM, GDN_CONV_DIM + GDN_V_DIM), layer=layer)
            w_ba = jnp.zeros((d, 2 * LANES), F32)
            w_ba = w_ba.at[:, :GDN_V_HEADS].set(w_in[:, GDN_CONV_DIM + GDN_V_DIM:GDN_CONV_DIM + GDN_V_DIM + GDN_V_HEADS])
            w_ba = w_ba.at[:, LANES:LANES + GDN_V_HEADS].set(w_in[:, GDN_CONV_DIM + GDN_V_DIM + GDN_V_HEADS:])
            ba = matmul(hn, wb(w_ba), F32)
            beta, gcum = gdn_gates(ba.reshape(b, lp, 2 * LANES), gdn_a_log[layer], gdn_dt_bias[layer])
            o = gdn_mix(qkv.reshape(b, lp, GDN_CONV_DIM), z.reshape(b, lp, GDN_V_DIM), gdn_conv_w[layer],
                        beta[..., :GDN_V_HEADS], gcum[..., :GDN_V_HEADS], gdn_o_norm[layer])
            mix = matmul(o.reshape(m, GDN_V_DIM), w_out_b, BF16, layer=layer)
        else:
            j = layer - N_A_LAYERS
            lambda_init = 0.8 - 0.6 * math.exp(-0.3 * layer)
            q = matmul_f32w(hn, diff_w_q, BF16, layer=j)
            o = diff_attention(q.reshape(b, lp, DIFF_Q_DIM), kv_k, kv_v, rope_tabs, diff_lambda[j],
                               diff_subln[j], lambda_init)
            mix = matmul(o.reshape(m, DIFF_HEADS * DIFF_V_DIM), w_o_b, BF16, layer=j)
        h, hn = resid_norm(h, mix, norm_gains[layer, 1], norm_gains[layer, 2:3])
        up = matmul_f32w(hn, mlp_w_up, BF16, relu2=True, layer=layer)
        ff = matmul(up, w_down_b, BF16, layer=layer)
        if layer == N_A_LAYERS - 1:
            g_next = jnp.stack([norm_gains[layer + 1, 0], kv_norm])
            h, hn, hkv = resid_norm(h, ff, norm_gains[layer, 3], g_next)
            kv_k = matmul_f32w(hkv, w_kv, BF16, cols=(0, DIFF_Q_DIM)).reshape(b, lp, DIFF_Q_DIM)
            kv_v = matmul_f32w(hkv, w_kv, BF16, cols=(DIFF_Q_DIM, w_kv.shape[1])).reshape(b, lp, DIFF_HEADS * DIFF_V_DIM)
        elif layer + 1 < DEPTH:
            h, hn = resid_norm(h, ff, norm_gains[layer, 3], norm_gains[layer + 1, 0:1])
        else:
            h, = resid_norm(h, ff, norm_gains[layer, 3], None)
    return h.reshape(b, lp, d)[:, LEAD + N_META:]
```

```python
import functools
import math

import jax
import jax.numpy as jnp
import numpy as np
from jax import lax
from jax.experimental import pallas as pl
from jax.experimental.pallas import tpu as pltpu

F32 = jnp.float32
BF16 = jnp.bfloat16

D_MODEL = 2048
DEPTH = 4
N_A_LAYERS = DEPTH // 2
N_META = 16
D_FF = 4 * D_MODEL
HEAD_DIM = 128
GDN_QK_HEADS = D_MODEL // HEAD_DIM
GDN_V_HEADS = 2 * GDN_QK_HEADS
GDN_QK_DIM = GDN_QK_HEADS * HEAD_DIM
GDN_V_DIM = GDN_V_HEADS * HEAD_DIM
GDN_CONV_DIM = 2 * GDN_QK_DIM + GDN_V_DIM
GDN_CONV_K = 4
CHUNK = 64
LEAD = CHUNK - N_META
PAIR = 2 * CHUNK
DIFF_HEADS = D_MODEL // 256
DIFF_V_DIM = 2 * HEAD_DIM
DIFF_Q_DIM = DIFF_HEADS * 2 * HEAD_DIM
ROT_DIM = HEAD_DIM // 4
ROPE_THETA = 500000.0
EPS = 1e-6
LANES = 128
GDN_PAIRS = 2
GDN_BLOCK_CHUNKS = 5
ATT_BLOCK = 512
MASK_VALUE = -1e30
VMEM_LIMIT_BYTES = 56 * 1024 * 1024


def _params(*sem):
    return pltpu.CompilerParams(dimension_semantics=sem, vmem_limit_bytes=VMEM_LIMIT_BYTES)


def _pick(n, cands):
    for c in cands:
        if n % c == 0:
            return c
    return n


def _rms(x):
    return x * lax.rsqrt(jnp.mean(x * x, axis=-1, keepdims=True) + EPS)


def _rmsnorm_kernel(x_ref, g_ref, o_ref):
    o_ref[...] = (_rms(x_ref[...]) * g_ref[...]).astype(o_ref.dtype)


def rmsnorm_rows(x, gain):
    m, d = x.shape
    tm = _pick(m, (640, 512, 320, 256, 128, 64, 32, 16, 8))
    return pl.pallas_call(
        _rmsnorm_kernel,
        grid=(m // tm,),
        in_specs=[pl.BlockSpec((tm, d), lambda i: (i, 0)), pl.BlockSpec((1, d), lambda i: (0, 0))],
        out_specs=pl.BlockSpec((tm, d), lambda i: (i, 0)),
        out_shape=jax.ShapeDtypeStruct((m, d), BF16),
        compiler_params=_params("parallel"),
        name="rmsnorm_rows",
    )(x, gain.reshape(1, d))


def _resid_norm_kernel(h_ref, y_ref, gp_ref, gn_ref, *out_refs, n_next):
    h_new = h_ref[...] + _rms(y_ref[...].astype(F32)) * gp_ref[...]
    out_refs[0][...] = h_new
    if n_next:
        hn = _rms(h_new)
        for j in range(n_next):
            out_refs[1 + j][...] = (hn * gn_ref[j:j + 1, :]).astype(BF16)


def resid_norm(h, y, g_post, g_next):
    m, d = h.shape
    n_next = 0 if g_next is None else g_next.shape[0]
    gn = jnp.zeros((1, d), F32) if g_next is None else g_next
    tm = _pick(m, (640, 512, 320, 256, 128, 64, 32, 16, 8))
    row = pl.BlockSpec((tm, d), lambda i: (i, 0))
    outs = pl.pallas_call(
        functools.partial(_resid_norm_kernel, n_next=n_next),
        grid=(m // tm,),
        in_specs=[row, row, pl.BlockSpec((1, d), lambda i: (0, 0)),
                  pl.BlockSpec(gn.shape, lambda i: (0, 0))],
        out_specs=[row] * (1 + n_next),
        out_shape=[jax.ShapeDtypeStruct((m, d), F32)] + [jax.ShapeDtypeStruct((m, d), BF16)] * n_next,
        compiler_params=_params("parallel"),
        name="resid_norm",
    )(h, y, g_post.reshape(1, d), gn)
    return outs


def _mm_kernel(x_ref, w_ref, o_ref, *acc, nk, relu2):
    def finish(r):
        if relu2:
            r = jnp.square(jnp.maximum(r, 0.0))
        o_ref[...] = r.astype(o_ref.dtype)

    part = jnp.dot(x_ref[...], w_ref[...], preferred_element_type=F32)
    if nk == 1:
        finish(part)
        return
    acc_ref, = acc
    k = pl.program_id(2)

    @pl.when(k == 0)
    def _():
        acc_ref[...] = part

    @pl.when(k > 0)
    def _():
        acc_ref[...] += part

    @pl.when(k == nk - 1)
    def _():
        finish(acc_ref[...])


def matmul(x, w, out_dtype, relu2=False, cols=None, layer=None):
    m, kdim = x.shape
    lo, hi = cols or (0, w.shape[-1])
    n = hi - lo
    tm = _pick(m, (1280, 1024, 640, 512, 320, 256, 128, 64, 32, 16, 8))
    tn = _pick(math.gcd(n, lo) if lo else n, (1024, 512, 256, 128))
    tk = kdim if kdim <= 4096 else 2048
    nk = kdim // tk
    assert lo % tn == 0 and n % tn == 0 and (w.ndim == 3) == (layer is not None)
    first = lo // tn
    if layer is None:
        w_spec = pl.BlockSpec((tk, tn), lambda i, j, k: (k, first + j))
    else:
        w_spec = pl.BlockSpec((None, tk, tn), lambda i, j, k: (layer, k, first + j))
    return pl.pallas_call(
        functools.partial(_mm_kernel, nk=nk, relu2=relu2),
        grid=(m // tm, n // tn, nk),
        in_specs=[pl.BlockSpec((tm, tk), lambda i, j, k: (i, k)), w_spec],
        out_specs=pl.BlockSpec((tm, tn), lambda i, j, k: (i, j)),
        out_shape=jax.ShapeDtypeStruct((m, n), out_dtype),
        scratch_shapes=[pltpu.VMEM((tm, tn), F32)] if nk > 1 else [],
        compiler_params=_params("parallel", "parallel", "arbitrary"),
        name="matmul",
    )(x, w)


def _mm_f32w_kernel(x_ref, w_ref, o_ref, wb_ref, *, relu2):
    @pl.when(pl.program_id(1) == 0)
    def _():
        wb_ref[...] = w_ref[...].astype(BF16)

    r = jnp.dot(x_ref[...], wb_ref[...], preferred_element_type=F32)
    if relu2:
        r = jnp.square(jnp.maximum(r, 0.0))
    o_ref[...] = r.astype(o_ref.dtype)


def matmul_f32w(x, w, out_dtype, relu2=False, cols=None, layer=None):
    m, kdim = x.shape
    lo, hi = cols or (0, w.shape[-1])
    n = hi - lo
    tm = _pick(m, (1280, 1024, 640, 512, 320, 256, 128, 64, 32, 16, 8))
    tn = _pick(math.gcd(n, lo) if lo else n, (1024, 512, 256, 128))
    assert lo % tn == 0 and n % tn == 0 and (w.ndim == 3) == (layer is not None)
    first = lo // tn
    if layer is None:
        w_spec = pl.BlockSpec((kdim, tn), lambda j, i: (0, first + j))
    else:
        w_spec = pl.BlockSpec((None, kdim, tn), lambda j, i: (layer, 0, first + j))
    return pl.pallas_call(
        functools.partial(_mm_f32w_kernel, relu2=relu2),
        grid=(n // tn, m // tm),
        in_specs=[pl.BlockSpec((tm, kdim), lambda j, i: (i, 0)), w_spec],
        out_specs=pl.BlockSpec((tm, tn), lambda j, i: (i, j)),
        out_shape=jax.ShapeDtypeStruct((m, n), out_dtype),
        scratch_shapes=[pltpu.VMEM((kdim, tn), BF16)],
        compiler_params=_params("parallel", "arbitrary"),
        name="matmul_f32w",
    )(x, w)


def _gates_kernel(ba_ref, alog_ref, dtb_ref, beta_ref, gcum_ref, *, n_chunks):
    row = lax.broadcasted_iota(jnp.int32, (CHUNK, CHUNK), 0)
    col = lax.broadcasted_iota(jnp.int32, (CHUNK, CHUNK), 1)
    tril = (row >= col).astype(F32)
    neg_rate = -jnp.exp(alog_ref[...])
    dtb = dtb_ref[...]

    def body(c, carry):
        r0 = pl.multiple_of(c * CHUNK, CHUNK)
        ba = ba_ref[0, pl.ds(r0, CHUNK), :]
        b, a = ba[:, :LANES], ba[:, LANES:]
        pos = r0 + lax.broadcasted_iota(jnp.int32, (CHUNK, LANES), 0)
        live = pos >= LEAD
        x = a + dtb
        softplus = jnp.maximum(x, 0.0) + jnp.log(1.0 + jnp.exp(-jnp.abs(x)))
        g = jnp.where(live, neg_rate * softplus, 0.0)
        beta_ref[0, pl.ds(r0, CHUNK), :] = jnp.where(live, 1.0 / (1.0 + jnp.exp(-b)), 0.0)
        gcum_ref[0, pl.ds(r0, CHUNK), :] = jnp.dot(tril, g, precision=lax.Precision.HIGHEST,
                                                   preferred_element_type=F32)
        return carry

    lax.fori_loop(0, n_chunks, body, 0)


def gdn_gates(ba, a_log, dt_bias):
    b, lp, _ = ba.shape
    pad = lambda t: jnp.zeros((1, LANES), F32).at[0, :GDN_V_HEADS].set(t.astype(F32))
    blk = pl.BlockSpec((1, lp, LANES), lambda i: (i, 0, 0))
    return pl.pallas_call(
        functools.partial(_gates_kernel, n_chunks=lp // CHUNK),
        grid=(b,),
        in_specs=[pl.BlockSpec((1, lp, 2 * LANES), lambda i: (i, 0, 0)),
                  pl.BlockSpec((1, LANES), lambda i: (0, 0)), pl.BlockSpec((1, LANES), lambda i: (0, 0))],
        out_specs=[blk, blk],
        out_shape=[jax.ShapeDtypeStruct((b, lp, LANES), F32)] * 2,
        compiler_params=_params("parallel"),
        name="gdn_gates",
    )(ba, pad(a_log), pad(dt_bias))


def _silu(x):
    return x / (1.0 + jnp.exp(-x))


def _gdn_masks():
    ri, ci = np.indices((PAIR, PAIR))
    same_head = (ri // CHUNK) == (ci // CHUNK)

    def sub_blocks(blk):
        return ((ri // blk) % 2 == 1) & ((ci // blk) % 2 == 0) & ((ri // (2 * blk)) == (ci // (2 * blk)))

    levels = []
    blk = 2
    while blk < CHUNK:
        levels.append(sub_blocks(blk))
        blk *= 2
    f32_masks = np.stack([ri == ci, ri != ci, sub_blocks(1)]).astype(np.float32)
    causal = np.where(same_head & (ri >= ci), 0.0, -np.inf).astype(np.float32)
    rr, cc = np.indices((PAIR, 2 * HEAD_DIM))
    own_cols = (rr // CHUNK) == (cc // HEAD_DIM)
    return (jnp.asarray(np.concatenate([f32_masks, causal[None]])), jnp.asarray(np.stack(levels), BF16),
            jnp.asarray(own_cols, BF16))


def _gdn_step(q_ref, k_ref, v_ref, z_ref, qp_ref, kp_ref, vp_ref, wq_ref, wk_ref, wv_ref, brow_ref, grow_ref,
              onorm_ref, mf_ref, ml_ref, own_ref, o_ref, s_ref, xf_s, wr, rd, first_block,
              do_prep=True, do_recur=True):
    cb, pairs = GDN_BLOCK_CHUNKS, GDN_PAIRS
    u_w, wq_w, aqk_w, kdt_w, keep_w = wr
    u_r, wq_r, aqk_r, kdt_r, keep_r = rd
    eye, off_diag, sub1, causal = mf_ref[0], mf_ref[1], mf_ref[2], mf_ref[3]
    head1_lane = lax.broadcasted_iota(jnp.int32, (1, 2 * HEAD_DIM), 1) >= HEAD_DIM
    head1_row = lax.broadcasted_iota(jnp.int32, (PAIR, 1), 0) >= CHUNK

    def dot(a, b):
        return jnp.dot(a, b, preferred_element_type=F32)

    def dot_t(a, b):
        return lax.dot_general(a, b, (((1,), (1,)), ((), ())), preferred_element_type=F32)

    def to_col(row):
        return jnp.sum(row * eye, axis=-1, keepdims=True)

    qw, vw = pairs * HEAD_DIM, 2 * pairs * HEAD_DIM
    seen_rows = jnp.where(first_block, 0.0, 1.0)
    for ref, halo_ref, lo, width in ((q_ref, qp_ref, 0, qw), (k_ref, kp_ref, qw, qw), (v_ref, vp_ref, 2 * qw, vw)):
        if do_prep:
            xf_s[0:8, lo:lo + width] = halo_ref[0].astype(F32)[8:, :] * seen_rows
            xf_s[8:, lo:lo + width] = ref[0].astype(F32)

    def conv_part(w_ref, c, lo, wlo, width):
        w = w_ref[:, wlo:wlo + width]
        y = None
        for s in range(GDN_CONV_K):
            r0 = 8 + c * CHUNK - s
            term = xf_s[r0:r0 + CHUNK, lo:lo + width] * w[GDN_CONV_K - 1 - s:GDN_CONV_K - s, :]
            y = term if y is None else y + term
        return _silu(y)

    units = [(c, p) for c in range(cb) for p in range(pairs)]
    st = dict(a_b=[], k2=[], q2=[], kb=[], v2=[], beta=[], g_col=[], g_last=[], ts=[], tbs=[], fts=[])

    def stage_a():
        for n, (c, p) in enumerate(units):
            q = conv_part(wq_ref, c, p * HEAD_DIM, p * HEAD_DIM, HEAD_DIM)
            k = conv_part(wk_ref, c, qw + p * HEAD_DIM, p * HEAD_DIM, HEAD_DIM)
            v = conv_part(wv_ref, c, 2 * qw + 2 * p * HEAD_DIM, 2 * p * HEAD_DIM, 2 * HEAD_DIM)
            q = q * lax.rsqrt(jnp.sum(q * q, axis=-1, keepdims=True) + EPS) * HEAD_DIM ** -0.5
            k = k * lax.rsqrt(jnp.sum(k * k, axis=-1, keepdims=True) + EPS)
            g_row = grow_ref[0, p, 0, c:c + 1, :]
            g_col = to_col(g_row)
            beta = to_col(brow_ref[0, p, 0, c:c + 1, :])
            gl = [g_row[:, (hh + 1) * CHUNK - 1:(hh + 1) * CHUNK] for hh in range(2)]
            g_last = jnp.where(head1_row, gl[1], gl[0])
            keep_w[n] = jnp.where(head1_lane, jnp.exp(gl[1]), jnp.exp(gl[0]))
            decay = jnp.exp((g_col - g_row) + causal)
            k2 = jnp.concatenate([k, k], axis=0)
            q2 = jnp.concatenate([q, q], axis=0)
            kb = k2 * beta
            k2b = k2.astype(BF16)
            a_kk = dot_t(kb.astype(BF16), k2b) * decay * off_diag
            aqk_w[n] = (dot_t(q2.astype(BF16), k2b) * decay).astype(BF16)
            st["a_b"].append(a_kk.astype(BF16))
            st["ts"].append(eye - a_kk * sub1)
            st["k2"].append(k2)
            st["q2"].append(q2)
            st["kb"].append(kb)
            st["beta"].append(beta)
            st["v2"].append(jnp.concatenate([v[:, :HEAD_DIM], v[:, HEAD_DIM:]], axis=0))
            st["g_col"].append(g_col)
            st["g_last"].append(g_last)

    def level_ft(level):
        mask = ml_ref[level]
        st["tbs"] = [t.astype(BF16) for t in st["ts"]]
        st["fts"] = [dot(a * mask, tb).astype(BF16) for a, tb in zip(st["a_b"], st["tbs"])]

    def level_t(blk):
        st["ts"] = [t - dot(tb, ft) for t, tb, ft in zip(st["ts"], st["tbs"], st["fts"])]

    def stage_sol():
        for n in range(len(units)):
            e_g = jnp.exp(st["g_col"][n])
            rhs = jnp.concatenate([st["v2"][n] * st["beta"][n], st["kb"][n] * e_g], axis=1).astype(BF16)
            sol = dot(st["ts"][n].astype(BF16), rhs)
            u_w[n] = sol[:, :HEAD_DIM]
            w = sol[:, HEAD_DIM:].astype(BF16)
            qe = (st["q2"][n] * e_g).astype(BF16)
            wq_w[n, 0] = jnp.concatenate([w[:CHUNK], qe[:CHUNK]], axis=0)
            wq_w[n, 1] = jnp.concatenate([w[CHUNK:], qe[CHUNK:]], axis=0)
            kdt_w[n] = (st["k2"][n] * jnp.exp(st["g_last"][n] - st["g_col"][n])).T.astype(BF16)

    onorm = onorm_ref[...]
    seq = {}

    def read_state(c):
        sb = [s_ref[p].astype(BF16) for p in range(pairs)]
        seq["r"] = [[dot(wq_r[c * pairs + p, hh], sb[p][:, hh * HEAD_DIM:(hh + 1) * HEAD_DIM]) for hh in range(2)]
                    for p in range(pairs)]

    def write_state(c):
        r = seq["r"]
        for p in range(pairs):
            n = c * pairs + p
            u = u_r[n]
            v_new = jnp.concatenate([u[:CHUNK] - r[p][0][:CHUNK], u[CHUNK:] - r[p][1][:CHUNK]], axis=0).astype(BF16)
            o_intra = dot(aqk_r[n], v_new)
            v_blk = jnp.concatenate([v_new, v_new], axis=1) * own_ref[...]
            s_ref[p] = s_ref[p] * keep_r[n] + dot(kdt_r[n], v_blk)
            for hh in range(2):
                lo = (2 * p + hh) * HEAD_DIM
                rows = slice(c * CHUNK, (c + 1) * CHUNK)
                o = r[p][hh][CHUNK:] + o_intra[hh * CHUNK:(hh + 1) * CHUNK]
                o = _rms(o) * onorm * _silu(z_ref[0, rows, lo:lo + HEAD_DIM].astype(F32))
                o_ref[0, rows, lo:lo + HEAD_DIM] = o.astype(o_ref.dtype)

    prep = [stage_a]
    for level in range(ml_ref.shape[0]):
        prep += [functools.partial(level_ft, level), functools.partial(level_t, level)]
    prep.append(stage_sol)
    recur = [f for c in range(cb) for f in (functools.partial(read_state, c), functools.partial(write_state, c))]
    prep = prep if do_prep else []
    recur = recur if do_recur else []
    for i in range(max(len(prep), len(recur))):
        if i < len(recur):
            recur[i]()
        if i < len(prep):
            prep[i]()


def _gdn_kernel(*refs, n_time_blocks):
    ins, o_ref, (s_ref, xf_s), bufs = refs[:16], refs[16], refs[17:19], refs[19:]
    sets = bufs[:5], bufs[5:]
    t = pl.program_id(2)
    last = n_time_blocks
    step = functools.partial(_gdn_step, *ins, o_ref, s_ref, xf_s)

    @pl.when(t == 0)
    def _():
        s_ref[...] = jnp.zeros_like(s_ref)
        step(sets[0], sets[1], True, do_recur=False)

    for parity in range(2):
        @pl.when((t > 0) & (t < last) & (t % 2 == parity))
        def _():
            step(sets[parity], sets[1 - parity], False)

    @pl.when(t == last)
    def _():
        step(sets[last % 2], sets[1 - last % 2], False, do_prep=False)


def gdn_mix(qkv, z, conv_w, beta, gcum, o_norm):
    b, lp, _ = qkv.shape
    nc = lp // CHUNK
    cb, pairs = GDN_BLOCK_CHUNKS, GDN_PAIRS
    assert nc % cb == 0 and GDN_QK_HEADS % pairs == 0
    nt, rb, ng = nc // cb, cb * CHUNK, GDN_QK_HEADS // pairs
    rows = lambda t: (t.reshape(b, nt, cb, CHUNK, GDN_QK_HEADS, 2).transpose(0, 4, 1, 2, 5, 3)
                      .reshape(b, GDN_QK_HEADS, nt, cb, PAIR))
    qw, vw = pairs * HEAD_DIM, 2 * pairs * HEAD_DIM
    k_off, v_off = GDN_QK_DIM // qw, 2 * GDN_QK_DIM // vw
    cur = lambda t: jnp.minimum(t, nt - 1)
    prv = lambda t: jnp.maximum(t - 1, 0)
    seq = lambda w, off: pl.BlockSpec((1, rb, w), lambda i, j, t, off=off: (i, cur(t), off + j))
    halo = lambda w, off: pl.BlockSpec(
        (1, 16, w), lambda i, j, t, off=off: (i, jnp.maximum(cur(t) * (rb // 16) - 1, 0), off + j))
    cw = lambda w, off: pl.BlockSpec((GDN_CONV_K, w), lambda i, j, t, off=off: (0, off + j))
    gate = pl.BlockSpec((1, pairs, 1, cb, PAIR), lambda i, j, t: (i, j, cur(t), 0, 0))
    n_units = cb * pairs
    masks = _gdn_masks()
    handover = [pltpu.VMEM((n_units, PAIR, HEAD_DIM), F32),
                pltpu.VMEM((n_units, 2, PAIR, HEAD_DIM), BF16),
                pltpu.VMEM((n_units, PAIR, PAIR), BF16),
                pltpu.VMEM((n_units, HEAD_DIM, PAIR), BF16),
                pltpu.VMEM((n_units, 1, 2 * HEAD_DIM), F32)]
    return pl.pallas_call(
        functools.partial(_gdn_kernel, n_time_blocks=nt),
        grid=(b, ng, nt + 1),
        in_specs=[seq(qw, 0), seq(qw, k_off), seq(vw, v_off),
                  pl.BlockSpec((1, rb, vw), lambda i, j, t: (i, prv(t), j)),
                  halo(qw, 0), halo(qw, k_off), halo(vw, v_off),
                  cw(qw, 0), cw(qw, k_off), cw(vw, v_off),
                  gate, gate, pl.BlockSpec((1, HEAD_DIM), lambda i, j, t: (0, 0))]
                 + [pl.BlockSpec(mk.shape, lambda i, j, t, nd=mk.ndim: (0,) * nd) for mk in masks],
        out_specs=pl.BlockSpec((1, rb, vw), lambda i, j, t: (i, prv(t), j)),
        out_shape=jax.ShapeDtypeStruct((b, lp, GDN_V_DIM), BF16),
        scratch_shapes=[pltpu.VMEM((pairs, HEAD_DIM, 2 * HEAD_DIM), F32),
                        pltpu.VMEM((rb + 8, 2 * qw + vw), F32)] + handover + handover,
        compiler_params=_params("parallel", "parallel", "arbitrary"),
        name="gdn_mix",
    )(qkv, qkv, qkv, z, qkv, qkv, qkv, conv_w, conv_w, conv_w, rows(beta), rows(gcum),
      o_norm.reshape(1, HEAD_DIM), *masks)


def _rope(x, cos, sin_lo, sin_hi):
    half = ROT_DIM // 2
    return x * cos + pltpu.roll(x, half, 1) * sin_hi + pltpu.roll(x, HEAD_DIM - half, 1) * sin_lo


def _attn_kernel(q_ref, k_ref, v_ref, cos_ref, slo_ref, shi_ref, lam_ref, subln_ref, o_ref,
                 kt_s, kh_s, m_s, l_s, acc_s, sc_s, *, n_blocks, lambda_init):
    tb = ATT_BLOCK
    lam = lam_ref[...]
    lam_val = (jnp.exp(jnp.sum(lam[0:1] * lam[1:2], axis=-1, keepdims=True))
               - jnp.exp(jnp.sum(lam[2:3] * lam[3:4], axis=-1, keepdims=True)) + lambda_init)
    subln = subln_ref[...]
    maps = range(2)

    def roped(ref, r0, n, scale):
        x = ref[0, pl.ds(r0, n), :].astype(F32)
        tabs = (cos_ref[pl.ds(r0, n), :], slo_ref[pl.ds(r0, n), :], shi_ref[pl.ds(r0, n), :])
        return [_rope(x[:, m * HEAD_DIM:(m + 1) * HEAD_DIM], *tabs) * scale for m in maps]

    def dot(a, b):
        return jnp.dot(a, b, preferred_element_type=F32)

    def dot_t(a, b):
        return lax.dot_general(a, b, (((1,), (1,)), ((), ())), preferred_element_type=F32)

    kh = roped(k_ref, 0, CHUNK, 1.0)
    for m in maps:
        kh_s[:, m * HEAD_DIM:(m + 1) * HEAD_DIM] = kh[m].astype(BF16)
    for i in range(n_blocks):
        kb = roped(k_ref, CHUNK + i * tb, tb, 1.0)
        for m in maps:
            kt_s[i, m * HEAD_DIM:(m + 1) * HEAD_DIM, :] = kb[m].T.astype(BF16)

    def start(qm, nq, qrow0, causal_head):
        kpos = lax.broadcasted_iota(jnp.int32, (nq, CHUNK), 1)
        ok = kpos >= LEAD
        if causal_head:
            ok = ok & (kpos <= qrow0 + lax.broadcasted_iota(jnp.int32, (nq, CHUNK), 0))
        vh = v_ref[0, pl.ds(0, CHUNK), :]
        s = [jnp.where(ok, dot_t(qm[m], kh_s[:, m * HEAD_DIM:(m + 1) * HEAD_DIM]), MASK_VALUE) for m in maps]
        mx = [jnp.max(s[m], axis=-1, keepdims=True) for m in maps]
        p = [jnp.exp2(s[m] - mx[m]) for m in maps]
        lane0 = lax.broadcasted_iota(jnp.int32, (nq, LANES), 1) == 0
        for m in maps:
            m_s[m, pl.ds(0, nq), :] = jnp.broadcast_to(mx[m], (nq, LANES))
            l_s[m, pl.ds(0, nq), :] = jnp.where(lane0, jnp.sum(p[m], axis=-1, keepdims=True), 0.0)
            acc_s[m, pl.ds(0, nq), :] = dot(p[m].astype(BF16), vh)

    def lane_cols(x):
        return [x[:, c * LANES:(c + 1) * LANES] for c in range(x.shape[1] // LANES)]

    def scores(qm, j):
        return [dot(qm[m], kt_s[j, m * HEAD_DIM:(m + 1) * HEAD_DIM, :]) for m in maps]

    def update(s, j, diag):
        vb = v_ref[0, pl.ds(pl.multiple_of(CHUNK + j * tb, CHUNK), tb), :]
        if diag:
            ok = (lax.broadcasted_iota(jnp.int32, (tb, tb), 1) <= lax.broadcasted_iota(jnp.int32, (tb, tb), 0))
            s = [jnp.where(ok, s[m], MASK_VALUE) for m in maps]
        sc = [lane_cols(s[m]) for m in maps]
        m_old = [m_s[m] for m in maps]
        m_new = [jnp.maximum(m_old[m], jnp.max(functools.reduce(jnp.maximum, sc[m]), axis=-1, keepdims=True))
                 for m in maps]
        pc = [[jnp.exp2(x - m_new[m]) for x in sc[m]] for m in maps]
        alpha = [jnp.exp2(m_old[m] - m_new[m]) for m in maps]
        pv = [dot(jnp.concatenate(pc[m], axis=1).astype(BF16), vb) for m in maps]
        for m in maps:
            m_s[m] = m_new[m]
            l_s[m] = alpha[m] * l_s[m] + functools.reduce(jnp.add, pc[m])
            acc_s[m] = jnp.concatenate([alpha[m]] * (DIFF_V_DIM // LANES), axis=1) * acc_s[m] + pv[m]

    def finish(r0, nq):
        l = [jnp.sum(l_s[m, pl.ds(0, nq), :], axis=-1, keepdims=True) for m in maps]
        o = acc_s[0, pl.ds(0, nq), :] / l[0] - lam_val * (acc_s[1, pl.ds(0, nq), :] / l[1])
        o_ref[0, pl.ds(r0, nq), :] = (_rms(o) * subln * (1.0 - lambda_init)).astype(o_ref.dtype)

    scale = HEAD_DIM ** -0.5 * math.log2(math.e)
    bf = lambda xs: [x.astype(BF16) for x in xs]
    start(bf(roped(q_ref, 0, CHUNK, scale)), CHUNK, 0, True)
    finish(0, CHUNK)

    def q_block(qi, carry):
        r0 = pl.multiple_of(CHUNK + qi * tb, CHUNK)
        qm = bf(roped(q_ref, r0, tb, scale))
        start(qm, tb, r0, False)

        def produce(slot, j):
            for m, s in enumerate(scores(qm, j)):
                sc_s[slot, m] = s

        def consume(slot, j, diag):
            update([sc_s[slot, m] for m in maps], j, diag)

        produce(0, 0)

        def kv_pair(jj, c2):
            j = 2 * jj
            produce(1, j + 1)
            consume(0, j, False)
            produce(0, j + 2)
            consume(1, j + 1, False)
            return c2

        lax.fori_loop(0, qi // 2, kv_pair, 0)

        @pl.when(qi % 2 == 0)
        def _():
            consume(0, qi, True)

        @pl.when(qi % 2 == 1)
        def _():
            produce(1, qi)
            consume(0, qi - 1, False)
            consume(1, qi, True)

        finish(r0, tb)
        return carry

    lax.fori_loop(0, n_blocks, q_block, 0)


def diff_attention(q, k, v, rope_tabs, lam, subln, lambda_init):
    b, lp, _ = q.shape
    n_blocks = (lp - CHUNK) // ATT_BLOCK
    assert CHUNK + n_blocks * ATT_BLOCK == lp
    head = pl.BlockSpec((1, lp, 2 * HEAD_DIM), lambda i, j: (i, 0, j))
    tab = pl.BlockSpec((lp, HEAD_DIM), lambda i, j: (0, 0))
    return pl.pallas_call(
        functools.partial(_attn_kernel, n_blocks=n_blocks, lambda_init=lambda_init),
        grid=(b, DIFF_HEADS),
        in_specs=[head, head, head, tab, tab, tab,
                  pl.BlockSpec((4, HEAD_DIM), lambda i, j: (0, 0)),
                  pl.BlockSpec((1, DIFF_V_DIM), lambda i, j: (0, 0))],
        out_specs=head,
        out_shape=jax.ShapeDtypeStruct((b, lp, DIFF_HEADS * DIFF_V_DIM), BF16),
        scratch_shapes=[pltpu.VMEM((n_blocks, 2 * HEAD_DIM, ATT_BLOCK), BF16),
                        pltpu.VMEM((CHUNK, 2 * HEAD_DIM), BF16),
                        pltpu.VMEM((2, ATT_BLOCK, LANES), F32),
                        pltpu.VMEM((2, ATT_BLOCK, LANES), F32),
                        pltpu.VMEM((2, ATT_BLOCK, DIFF_V_DIM), F32),
                        pltpu.VMEM((2, 2, ATT_BLOCK, ATT_BLOCK), F32)],
        compiler_params=_params("parallel", "parallel"),
        name="diff_attention",
    )(q, k, v, *rope_tabs, lam, subln.reshape(1, DIFF_V_DIM))


def _rope_tables(lp):
    half = ROT_DIM // 2
    pos = jnp.maximum(jnp.arange(lp, dtype=F32) - LEAD, 0.0)
    inv_freq = ROPE_THETA ** (-jnp.arange(0, ROT_DIM, 2, dtype=F32) / ROT_DIM)
    ang = pos[:, None] * inv_freq[None, :]
    c, s = jnp.cos(ang), jnp.sin(ang)
    zeros = jnp.zeros((lp, HEAD_DIM - ROT_DIM), F32)
    cos = jnp.concatenate([c, c, jnp.ones_like(zeros)], axis=1)
    sin_lo = jnp.concatenate([-s, jnp.zeros((lp, half), F32), zeros], axis=1)
    sin_hi = jnp.concatenate([jnp.zeros((lp, half), F32), s, zeros], axis=1)
    return cos, sin_lo, sin_hi


def kernel(x, meta_tokens, norm_gains, mlp_w_up, mlp_w_down, gdn_w_in, gdn_conv_w, gdn_a_log, gdn_dt_bias,
           gdn_o_norm, gdn_w_out, kv_norm, w_kv, diff_w_q, diff_lambda, diff_subln, diff_w_o):
    b, seq, d = x.shape
    lp = LEAD + N_META + seq
    m = b * lp
    meta = jnp.broadcast_to(meta_tokens.astype(x.dtype)[None], (b, N_META, d))
    h = jnp.concatenate([jnp.zeros((b, LEAD, d), x.dtype), meta, x], axis=1).reshape(m, d)
    rope_tabs = _rope_tables(lp)
    wb = lambda w: w.astype(BF16)
    w_down_b, w_out_b, w_o_b, w_in_b = wb(mlp_w_down), wb(gdn_w_out), wb(diff_w_o), wb(gdn_w_in)

    hn = rmsnorm_rows(h, norm_gains[0, 0])
    kv_k = kv_v = None
    for layer in range(DEPTH):
        if layer < N_A_LAYERS:
            w_in = gdn_w_in[layer]
            qkv = matmul(hn, w_in_b, BF16, cols=(0, GDN_CONV_DIM), layer=layer)
            z = matmul(hn, w_in_b, BF16, cols=(GDN_CONV_DIM, GDN_CONV_DIM + GDN_V_DIM), layer=layer)
            w_ba = jnp.zeros((d, 2 * LANES), F32)
            w_ba = w_ba.at[:, :GDN_V_HEADS].set(w_in[:, GDN_CONV_DIM + GDN_V_DIM:GDN_CONV_DIM + GDN_V_DIM + GDN_V_HEADS])
            w_ba = w_ba.at[:, LANES:LANES + GDN_V_HEADS].set(w_in[:, GDN_CONV_DIM + GDN_V_DIM + GDN_V_HEADS:])
            ba = matmul(hn, wb(w_ba), F32)
            beta, gcum = gdn_gates(ba.reshape(b, lp, 2 * LANES), gdn_a_log[layer], gdn_dt_bias[layer])
            o = gdn_mix(qkv.reshape(b, lp, GDN_CONV_DIM), z.reshape(b, lp, GDN_V_DIM), gdn_conv_w[layer],
                        beta[..., :GDN_V_HEADS], gcum[..., :GDN_V_HEADS], gdn_o_norm[layer])
            mix = matmul(o.reshape(m, GDN_V_DIM), w_out_b, BF16, layer=layer)
        else:
            j = layer - N_A_LAYERS
            lambda_init = 0.8 - 0.6 * math.exp(-0.3 * layer)
            q = matmul_f32w(hn, diff_w_q, BF16, layer=j)
            o = diff_attention(q.reshape(b, lp, DIFF_Q_DIM), kv_k, kv_v, rope_tabs, diff_lambda[j],
                               diff_subln[j], lambda_init)
            mix = matmul(o.reshape(m, DIFF_HEADS * DIFF_V_DIM), w_o_b, BF16, layer=j)
        h, hn = resid_norm(h, mix, norm_gains[layer, 1], norm_gains[layer, 2:3])
        up = matmul_f32w(hn, mlp_w_up, BF16, relu2=True, layer=layer)
        ff = matmul(up, w_down_b, BF16, layer=layer)
        if layer == N_A_LAYERS - 1:
            g_next = jnp.stack([norm_gains[layer + 1, 0], kv_norm])
            h, hn, hkv = resid_norm(h, ff, norm_gains[layer, 3], g_next)
            kv_k = matmul_f32w(hkv, w_kv, BF16, cols=(0, DIFF_Q_DIM)).reshape(b, lp, DIFF_Q_DIM)
            kv_v = matmul_f32w(hkv, w_kv, BF16, cols=(DIFF_Q_DIM, w_kv.shape[1])).reshape(b, lp, DIFF_HEADS * DIFF_V_DIM)
        elif layer + 1 < DEPTH:
            h, hn = resid_norm(h, ff, norm_gains[layer, 3], norm_gains[layer + 1, 0:1])
        else:
            h, = resid_norm(h, ff, norm_gains[layer, 3], None)
    return h.reshape(b, lp, d)[:, LEAD + N_META:]
```
